```python
import math
import jax, jax.numpy as jnp
from jax import lax
import numpy as np

D_MODEL = 2048
BATCH = 1
SEQ = 8192
DEPTH = 2

D_MIX = D_MODEL
GROUP = D_MIX // 4
EPS = 1e-6

A_HEADS = 4
A_CH = GROUP // A_HEADS
A_CHUNK = 128
A_IN = 2 * GROUP

B_HEADS = 8
B_HD = GROUP // B_HEADS
B_W_RANK = 32
B_A_RANK = 32
B_G_RANK = 96
B_LNX_EPS = 64e-5
B_IN = 3 * GROUP + B_W_RANK + B_A_RANK + B_G_RANK
B_SPLITS = [GROUP, 2 * GROUP, 3 * GROUP, 3 * GROUP + B_W_RANK, 3 * GROUP + B_W_RANK + B_A_RANK]

C_HEADS = 8
C_HD = GROUP // C_HEADS
C_BLOCK = 256
C_TOPK = 3
C_QBLOCK = 128
ROPE_THETA = 10000.0
C_IN = 3 * GROUP

D_HEADS = 4
D_DK = 64
D_DV = GROUP // D_HEADS
D_GATE_RANK = 16
D_GATE_TEMP = 16.0
D_CHUNK = 64
D_IN = 2 * D_HEADS * D_DK + GROUP + D_GATE_RANK + GROUP
D_SPLITS = [D_HEADS * D_DK, 2 * D_HEADS * D_DK, 2 * D_HEADS * D_DK + GROUP, 2 * D_HEADS * D_DK + GROUP + D_GATE_RANK]

N_IN = A_IN + B_IN + C_IN + D_IN
IN_SPLITS = [A_IN, A_IN + B_IN, A_IN + B_IN + C_IN]

D_FF = 5632
CONV_W = 3

kernel_name = "hymba_style_gmlp_rwkv7_moba_gla_convffn"

F32 = jnp.float32


def _rmsnorm(x, g):
    xf = x.astype(F32)
    y = xf * lax.rsqrt(jnp.mean(xf * xf, axis=-1, keepdims=True) + EPS)
    return (y * g.astype(F32)).astype(x.dtype)


def _rope(x, pos):
    half = x.shape[-1] // 2
    inv = ROPE_THETA ** (-jnp.arange(half, dtype=F32) / half)
    ang = pos.astype(F32)[:, None] * inv[None, :]
    cos = jnp.cos(ang)[None, :, None, :]
    sin = jnp.sin(ang)[None, :, None, :]
    x1, x2 = x[..., :half], x[..., half:]
    return jnp.concatenate([x1 * cos - x2 * sin, x1 * sin + x2 * cos], axis=-1)


def _gmlp_mixer(p, ln_g, ln_b, ws, bs):
    b, s, _ = p.shape
    z = jax.nn.gelu(p.astype(F32))
    u, v = jnp.split(z, 2, axis=-1)
    mu = jnp.mean(v, axis=-1, keepdims=True)
    var = jnp.mean(jnp.square(v - mu), axis=-1, keepdims=True)
    v = (v - mu) * lax.rsqrt(var + EPS) * ln_g + ln_b
    v = v.reshape(b, s // A_CHUNK, A_CHUNK, A_HEADS, A_CH)
    causal = jnp.tril(jnp.ones((A_CHUNK, A_CHUNK), dtype=bool))
    w_s = jnp.where(causal[None], ws.astype(F32), 0.0)
    mixed = jnp.einsum("hts,bnshc->bnthc", w_s, v) + bs.astype(F32).T[None, None, :, :, None]
    return (u * mixed.reshape(b, s, GROUP)).astype(p.dtype)


def _rwkv7_mixer(p, mu, w0, w2, a0, a2, g2, k_k, k_a, r_k, lnx_g, lnx_b):
    b, s, _ = p.shape
    dtype = p.dtype
    p = p.astype(F32)
    prev = jnp.pad(p, ((0, 0), (1, 0), (0, 0)))[:, :s]
    p = p + (prev - p) * mu
    r, k, v, xw, xa, xg = jnp.split(p, B_SPLITS, axis=-1)
    w = -jax.nn.softplus(-(w0 + jnp.tanh(xw) @ w2)) - 0.5
    decay = jnp.exp(-jnp.exp(w))
    a = jax.nn.sigmoid(a0 + xa @ a2)
    g = jax.nn.sigmoid(xg) @ g2

    def heads(t):
        return t.reshape(b, s, B_HEADS, B_HD)

    kk = heads(k * k_k)
    kk = kk / jnp.maximum(jnp.linalg.norm(kk, axis=-1, keepdims=True), 1e-12)
    k = k * (1.0 + (a - 1.0) * k_a)
    r_h, k_h, v_h, a_h, d_h = heads(r), heads(k), heads(v), heads(a), heads(decay)
    b_h = kk * a_h
    xs = tuple(jnp.moveaxis(t, 1, 0) for t in (r_h, d_h, k_h, v_h, kk, b_h))

    def step(state, inp):
        r_t, d_t, k_t, v_t, kk_t, b_t = inp
        sa = jnp.einsum("bhvk,bhk->bhv", state, -kk_t)
        state = (state * d_t[:, :, None, :] + sa[..., None] * b_t[:, :, None, :]
                 + v_t[..., None] * k_t[:, :, None, :])
        return state, jnp.einsum("bhvk,bhk->bhv", state, r_t)

    state0 = jnp.zeros((b, B_HEADS, B_HD, B_HD), F32)
    _, y = lax.scan(step, state0, xs)
    y = jnp.moveaxis(y, 0, 1)
    ym = jnp.mean(y, axis=-1, keepdims=True)
    yv = jnp.mean(jnp.square(y - ym), axis=-1, keepdims=True)
    y = ((y - ym) * lax.rsqrt(yv + B_LNX_EPS)).reshape(b, s, GROUP) * lnx_g + lnx_b
    bonus = jnp.sum(r_h * k_h * r_k, axis=-1, keepdims=True) * v_h
    y = y + bonus.reshape(b, s, GROUP)
    return (y * g).astype(dtype)


def _moba_mixer(p):
    b, s, _ = p.shape
    dtype = p.dtype
    p = p.astype(F32)
    q, k, v = jnp.split(p, 3, axis=-1)
    shape = (b, s, C_HEADS, C_HD)
    pos = jnp.arange(s)
    q = _rope(q.reshape(shape), pos) * (C_HD ** -0.5)
    k = _rope(k.reshape(shape), pos)
    v = v.reshape(shape)
    nb = -(-s // C_BLOCK)
    pad = nb * C_BLOCK - s

    def blocks(t):
        t = jnp.pad(t, ((0, 0), (0, pad), (0, 0), (0, 0)))
        return t.reshape(b, nb, C_BLOCK, C_HEADS, C_HD).transpose(0, 3, 1, 2, 4)

    kb, vb = blocks(k), blocks(v)
    kmean = jnp.mean(kb, axis=3)
    qt = q.transpose(0, 2, 1, 3)
    bscore = jnp.einsum("bhsd,bhnd->bhsn", qt, kmean)
    fully_past = jnp.arange(nb)[None, :] < (pos // C_BLOCK)[:, None]
    bscore = jnp.where(fully_past, bscore, -jnp.inf)
    n_sel = min(C_TOPK, nb)
    top_s, top_i = lax.top_k(bscore, n_sel)
    top_ok = jnp.isfinite(top_s)
    gather = jax.vmap(jax.vmap(lambda blk, idx: blk[idx]))
    own_off = jnp.arange(C_BLOCK)
    q_off = jnp.arange(C_QBLOCK)

    def attend(qi):
        start = qi * C_QBLOCK
        q_blk = lax.dynamic_slice_in_dim(qt, start, C_QBLOCK, axis=2)
        idx = lax.dynamic_slice_in_dim(top_i, start, C_QBLOCK, axis=2)
        ok = lax.dynamic_slice_in_dim(top_ok, start, C_QBLOCK, axis=2)
        k_sel = gather(kb, idx)
        v_sel = gather(vb, idx)
        s_sel = jnp.einsum("bhqd,bhqkld->bhqkl", q_blk, k_sel)
        s_sel = jnp.where(ok[..., None], s_sel, -jnp.inf).reshape(b, C_HEADS, C_QBLOCK, n_sel * C_BLOCK)
        own = start // C_BLOCK
        k_own = lax.dynamic_index_in_dim(kb, own, axis=2, keepdims=False)
        v_own = lax.dynamic_index_in_dim(vb, own, axis=2, keepdims=False)
        s_own = jnp.einsum("bhqd,bhld->bhql", q_blk, k_own)
        visible = (own * C_BLOCK + own_off)[None, :] <= (start + q_off)[:, None]
        s_own = jnp.where(visible, s_own, -jnp.inf)
        probs = jax.nn.softmax(jnp.concatenate([s_sel, s_own], axis=-1), axis=-1)
        p_sel = probs[..., :n_sel * C_BLOCK].reshape(b, C_HEADS, C_QBLOCK, n_sel, C_BLOCK)
        p_own = probs[..., n_sel * C_BLOCK:]
        return (jnp.einsum("bhqkl,bhqkld->bhqd", p_sel, v_sel)
                + jnp.einsum("bhql,bhld->bhqd", p_own, v_own))

    out = lax.map(attend, jnp.arange(s // C_QBLOCK))
    out = out.transpose(1, 0, 3, 2, 4).reshape(b, s, GROUP)
    return out.astype(dtype)


def _gla_mixer(p, gate_w2, gate_b, norm_g):
    b, s, _ = p.shape
    dtype = p.dtype
    p = p.astype(F32)
    q, k, v, xg, og = jnp.split(p, D_SPLITS, axis=-1)
    log_a = jax.nn.log_sigmoid(xg @ gate_w2 + gate_b) / D_GATE_TEMP
    nc = s // D_CHUNK

    def chunks(t, d):
        return t.reshape(b, nc, D_CHUNK, D_HEADS, d).transpose(1, 0, 3, 2, 4)

    qc = chunks(q * (D_DK ** -0.5), D_DK)
    kc = chunks(k, D_DK)
    vc = chunks(v, D_DV)
    gc = chunks(log_a, D_DK)
    causal = jnp.tril(jnp.ones((D_CHUNK, D_CHUNK), dtype=bool))

    def step(state, inp):
        q_c, k_c, v_c, g_c = inp
        cum = jnp.cumsum(g_c, axis=2)
        diff = cum[:, :, :, None, :] - cum[:, :, None, :, :]
        dec = jnp.exp(jnp.where(causal[:, :, None], diff, -jnp.inf))
        scores = jnp.einsum("bhtd,bhsd,bhtsd->bhts", q_c, k_c, dec)
        o = scores @ v_c + jnp.einsum("bhtd,bhde->bhte", q_c * jnp.exp(cum), state)
        last = cum[:, :, -1, :]
        state = (jnp.exp(last)[..., None] * state
                 + jnp.einsum("bhsd,bhse->bhde", k_c * jnp.exp(last[:, :, None, :] - cum), v_c))
        return state, o

    state0 = jnp.zeros((b, D_HEADS, D_DK, D_DV), F32)
    _, o = lax.scan(step, state0, (qc, kc, vc, gc))
    o = o.transpose(1, 0, 3, 2, 4).reshape(b, s, D_HEADS, D_DV)
    o = o * lax.rsqrt(jnp.mean(o * o, axis=-1, keepdims=True) + EPS) * norm_g
    return (o.reshape(b, s, GROUP) * jax.nn.silu(og)).astype(dtype)


def _conv_ffn(x, w_up, conv_w, conv_b, w_down):
    s = x.shape[1]
    h = x @ w_up
    hp = jnp.pad(h, ((0, 0), (CONV_W - 1, 0), (0, 0)))
    acc = conv_b
    for j in range(CONV_W):
        acc = acc + hp[:, j:j + s] * conv_w[j]
    gate, up = jnp.split(acc, 2, axis=-1)
    return (jax.nn.silu(gate) * up) @ w_down


def setup_inputs(seed: int = 0) -> dict:
    key = jax.random.key(seed)
    ks = iter(jax.random.split(key, 32))

    def nrm(shape, scale):
        return jax.random.normal(next(ks), shape, F32) * scale

    def gain(shape):
        return 1.0 + nrm(shape, 0.02)

    L = DEPTH
    return {
        "x": nrm((BATCH, SEQ, D_MODEL), 1.0),
        "mix_norm_g": gain((L, D_MODEL)),
        "w_in": nrm((L, D_MODEL, N_IN), D_MODEL ** -0.5),
        "a_ln_g": gain((L, GROUP)),
        "a_ln_b": nrm((L, GROUP), 0.02),
        "a_ws": nrm((L, A_HEADS, A_CHUNK, A_CHUNK), A_CHUNK ** -0.5),
        "a_bs": 1.0 + nrm((L, A_HEADS, A_CHUNK), 0.1),
        "b_mu": jax.random.uniform(next(ks), (L, B_IN), F32),
        "b_w0": jax.random.uniform(next(ks), (L, GROUP), F32, -6.0, -1.0),
        "b_w2": nrm((L, B_W_RANK, GROUP), 0.1),
        "b_a0": nrm((L, GROUP), 0.1),
        "b_a2": nrm((L, B_A_RANK, GROUP), 0.1),
        "b_g2": nrm((L, B_G_RANK, GROUP), B_G_RANK ** -0.5),
        "b_kk": 0.85 + nrm((L, GROUP), 0.05),
        "b_ka": 1.0 + nrm((L, GROUP), 0.05),
        "b_rk": nrm((L, B_HEADS, B_HD), 0.1),
        "b_lnx_g": gain((L, GROUP)),
        "b_lnx_b": nrm((L, GROUP), 0.02),
        "d_gate_w2": nrm((L, D_GATE_RANK, D_HEADS * D_DK), D_GATE_RANK ** -0.5),
        "d_gate_b": 2.0 + nrm((L, D_HEADS * D_DK), 0.5),
        "d_norm_g": gain((L, D_DV)),
        "w_out": nrm((L, D_MIX, D_MODEL), D_MIX ** -0.5),
        "ffn_norm_g": gain((L, D_MODEL)),
        "w_up": nrm((L, D_MODEL, 2 * D_FF), D_MODEL ** -0.5),
        "conv_w": nrm((L, CONV_W, 2 * D_FF), CONV_W ** -0.5),
        "conv_b": nrm((L, 2 * D_FF), 0.02),
        "w_down": nrm((L, D_FF, D_MODEL), D_FF ** -0.5),
        "final_norm_g": gain((D_MODEL,)),
    }


def reference(x, mix_norm_g, w_in, a_ln_g, a_ln_b, a_ws, a_bs, b_mu, b_w0, b_w2, b_a0, b_a2,
              b_g2, b_kk, b_ka, b_rk, b_lnx_g, b_lnx_b, d_gate_w2, d_gate_b, d_norm_g, w_out,
              ffn_norm_g, w_up, conv_w, conv_b, w_down, final_norm_g):
    for l in range(DEPTH):
        h = _rmsnorm(x, mix_norm_g[l])
        p = h @ w_in[l]
        pa, pb, pc, pd = jnp.split(p, IN_SPLITS, axis=-1)
        ya = _gmlp_mixer(pa, a_ln_g[l], a_ln_b[l], a_ws[l], a_bs[l])
        yb = _rwkv7_mixer(pb, b_mu[l], b_w0[l], b_w2[l], b_a0[l], b_a2[l], b_g2[l],
                          b_kk[l], b_ka[l], b_rk[l], b_lnx_g[l], b_lnx_b[l])
        yc = _moba_mixer(pc)
        yd = _gla_mixer(pd, d_gate_w2[l], d_gate_b[l], d_norm_g[l])
        x = x + jnp.concatenate([ya, yb, yc, yd], axis=-1) @ w_out[l]
        x = x + _conv_ffn(_rmsnorm(x, ffn_norm_g[l]), w_up[l], conv_w[l], conv_b[l], w_down[l])
    return _rmsnorm(x, final_norm_g)
```

```python
import functools
import math

import jax
import jax.numpy as jnp
import numpy as np
from jax import lax
from jax.experimental import pallas as pl
from jax.experimental.pallas import tpu as pltpu

F32 = jnp.float32
BF16 = jnp.bfloat16

EPS = 1e-6
GROUP = 512
LANES = 128
VMEM_LIMIT = 56 * 1024 * 1024

A_HEADS, A_CH, A_CHUNK = 4, 128, 128
B_HEADS, B_HD, B_CHUNK = 8, 64, 64
B_LNX_EPS = 64e-5
C_HEADS, C_HD, C_BLOCK, C_TOPK = 8, 64, 256, 3
ROPE_THETA = 10000.0
D_HEADS, D_DK, D_DV, D_CHUNK = 4, 64, 128, 64
D_GATE_TEMP = 16.0
D_FF = 5632
NEG = -1e30


def _params(*sem):
    return pltpu.CompilerParams(dimension_semantics=sem, vmem_limit_bytes=VMEM_LIMIT)


def _dot(a, b):
    return jnp.dot(a, b, preferred_element_type=F32)


def _dot_nt(a, b):
    return lax.dot_general(a, b, (((1,), (1,)), ((), ())), preferred_element_type=F32)


def _dot_tn(a, b):
    return lax.dot_general(a, b, (((0,), (0,)), ((), ())), preferred_element_type=F32)


def _split3(x):
    hi = x.astype(BF16)
    r1 = x - hi.astype(F32)
    mid = r1.astype(BF16)
    lo = (r1 - mid.astype(F32)).astype(BF16)
    return hi, mid, lo


def _dot3(sel, x):
    hi, mid, lo = _split3(x)
    return _dot(sel, hi) + _dot(sel, mid) + _dot(sel, lo)


def _dot3_nt(a, b):
    ah, am, al = _split3(a)
    bh, bm, bl = _split3(b)
    return (_dot_nt(ah, bh) + (_dot_nt(ah, bm) + _dot_nt(am, bh))
            + (_dot_nt(am, bm) + _dot_nt(ah, bl) + _dot_nt(al, bh)))


def _sigmoid(x):
    return 1.0 / (1.0 + jnp.exp(-x))


def _softplus(x):
    return jnp.maximum(x, 0.0) + jnp.log(1.0 + jnp.exp(-jnp.abs(x)))


def _iota2(shape, axis):
    return lax.broadcasted_iota(jnp.int32, shape, axis)


def _rmsnorm_kernel(x_ref, g_ref, o_ref):
    x = x_ref[...]
    ms = jnp.mean(x * x, axis=-1, keepdims=True)
    o_ref[...] = (x * lax.rsqrt(ms + EPS) * g_ref[...]).astype(o_ref.dtype)


def _rmsnorm(x, g, out_dtype, tm=512):
    m, d = x.shape
    return pl.pallas_call(
        _rmsnorm_kernel,
        grid=(m // tm,),
        in_specs=[pl.BlockSpec((tm, d), lambda i: (i, 0)),
                  pl.BlockSpec((1, d), lambda i: (0, 0))],
        out_specs=pl.BlockSpec((tm, d), lambda i: (i, 0)),
        out_shape=jax.ShapeDtypeStruct((m, d), out_dtype),
        compiler_params=_params("parallel"),
    )(x, g.reshape(1, d))


def _mm_kernel(x_ref, w_ref, o_ref):
    o_ref[...] = _dot(x_ref[...], w_ref[...]).astype(o_ref.dtype)


def _matmul(x, w, out_dtype, tm=512):
    m, k = x.shape
    n = w.shape[1]
    return pl.pallas_call(
        _mm_kernel,
        grid=(m // tm,),
        in_specs=[pl.BlockSpec((tm, k), lambda i: (i, 0)),
                  pl.BlockSpec((k, n), lambda i: (0, 0))],
        out_specs=pl.BlockSpec((tm, n), lambda i: (i, 0)),
        out_shape=jax.ShapeDtypeStruct((m, n), out_dtype),
        compiler_params=_params("parallel"),
    )(x, w)


def _mm_res_kernel(x_ref, w_ref, r_ref, o_ref):
    o_ref[...] = r_ref[...] + _dot(x_ref[...], w_ref[...])


def _matmul_res(x, w, res, tm=512, tn=512):
    m, k = x.shape
    n = w.shape[1]
    return pl.pallas_call(
        _mm_res_kernel,
        grid=(n // tn, m // tm),
        in_specs=[pl.BlockSpec((tm, k), lambda j, i: (i, 0)),
                  pl.BlockSpec((k, tn), lambda j, i: (0, j)),
                  pl.BlockSpec((tm, tn), lambda j, i: (i, j))],
        out_specs=pl.BlockSpec((tm, tn), lambda j, i: (i, j)),
        out_shape=jax.ShapeDtypeStruct((m, n), F32),
        compiler_params=_params("parallel", "parallel"),
    )(x, w, res)


def _outproj_kernel(ya_ref, yb_ref, yct_ref, yd_ref, w_ref, x_ref, o_ref):
    yc = yct_ref[...].T.astype(BF16)
    acc = _dot(ya_ref[...], w_ref[0:GROUP, :])
    acc += _dot(yb_ref[...], w_ref[GROUP:2 * GROUP, :])
    acc += _dot(yc, w_ref[2 * GROUP:3 * GROUP, :])
    acc += _dot(yd_ref[...], w_ref[3 * GROUP:4 * GROUP, :])
    o_ref[...] = x_ref[...] + acc


def _outproj(ya, yb, yct, yd, w, x, tm=512, tn=1024):
    m = x.shape[0]
    n = w.shape[1]
    row = pl.BlockSpec((tm, GROUP), lambda j, i: (i, 0))
    return pl.pallas_call(
        _outproj_kernel,
        grid=(n // tn, m // tm),
        in_specs=[row, row,
                  pl.BlockSpec((GROUP, tm), lambda j, i: (0, i)),
                  row,
                  pl.BlockSpec((4 * GROUP, tn), lambda j, i: (0, j)),
                  pl.BlockSpec((tm, tn), lambda j, i: (i, j))],
        out_specs=pl.BlockSpec((tm, tn), lambda j, i: (i, j)),
        out_shape=jax.ShapeDtypeStruct((m, n), F32),
        compiler_params=_params("parallel", "parallel"),
    )(ya, yb, yct, yd, w, x)


def _gmlp_kernel(p_ref, lng_ref, lnb_ref, ws_ref, bsb_ref, o_ref, *, nchunk):
    p = p_ref[...]
    z = 0.5 * p * (1.0 + jnp.tanh(math.sqrt(2.0 / math.pi) * (p + 0.044715 * (p * p * p))))
    u = z[:, :GROUP]
    v = z[:, GROUP:]
    mu = jnp.mean(v, axis=-1, keepdims=True)
    vc = v - mu
    var = jnp.mean(vc * vc, axis=-1, keepdims=True)
    vn = (vc * lax.rsqrt(var + EPS) * lng_ref[...] + lnb_ref[...]).astype(BF16)
    tri = _iota2((A_CHUNK, A_CHUNK), 1) <= _iota2((A_CHUNK, A_CHUNK), 0)
    for h in range(A_HEADS):
        w = jnp.where(tri, ws_ref[h], 0.0).astype(BF16)
        cs = slice(h * A_CH, (h + 1) * A_CH)
        for c in range(nchunk):
            rs = slice(c * A_CHUNK, (c + 1) * A_CHUNK)
            mixed = _dot(w, vn[rs, cs]) + bsb_ref[:, cs]
            o_ref[rs, cs] = (u[rs, cs] * mixed).astype(o_ref.dtype)


def _gmlp(pa, ln_g, ln_b, ws, bs, nchunk=4):
    s = pa.shape[0]
    tm = nchunk * A_CHUNK
    bsb = jnp.repeat(bs.T, A_CH, axis=1)
    vec = pl.BlockSpec((1, GROUP), lambda i: (0, 0))
    return pl.pallas_call(
        functools.partial(_gmlp_kernel, nchunk=nchunk),
        grid=(s // tm,),
        in_specs=[pl.BlockSpec((tm, 2 * GROUP), lambda i: (i, 0)), vec, vec,
                  pl.BlockSpec((A_HEADS, A_CHUNK, A_CHUNK), lambda i: (0, 0, 0)),
                  pl.BlockSpec((A_CHUNK, GROUP), lambda i: (0, 0))],
        out_specs=pl.BlockSpec((tm, GROUP), lambda i: (i, 0)),
        out_shape=jax.ShapeDtypeStruct((s, GROUP), BF16),
        compiler_params=_params("parallel"),
    )(pa, ln_g.reshape(1, GROUP), ln_b.reshape(1, GROUP), ws, bsb)


B_W = 3 * GROUP + 3 * LANES
B_TM = 512


def _rwkv_prep_kernel(p_ref, pprev_ref, mu_ref, w0_ref, w2_ref, a0_ref, a2_ref, g2_ref,
                      kkw_ref, ka_ref, rk_ref,
                      at_ref, bt_ref, kt_ref, rt_ref, v_ref, bd_ref, kd_ref, pc_ref, g_ref, bonus_ref):
    i = pl.program_id(0)
    p = p_ref[...]
    tm = p.shape[0]
    last = jnp.where(i == 0, 0.0, pprev_ref[7:8, :])
    prev = jnp.where(_iota2(p.shape, 0) == 0, last, pltpu.roll(p, 1, axis=0))
    xs = p + (prev - p) * mu_ref[...]
    r = xs[:, 0:GROUP]
    k = xs[:, GROUP:2 * GROUP]
    v = xs[:, 2 * GROUP:3 * GROUP]
    xw = xs[:, 3 * GROUP:3 * GROUP + LANES]
    xa = xs[:, 3 * GROUP + LANES:3 * GROUP + 2 * LANES]
    xg = xs[:, 3 * GROUP + 2 * LANES:3 * GROUP + 3 * LANES]
    w = -_softplus(-(w0_ref[...] + _dot(jnp.tanh(xw).astype(BF16), w2_ref[...]))) - 0.5
    logd = -jnp.exp(w)
    a = _sigmoid(a0_ref[...] + _dot(xa.astype(BF16), a2_ref[...]))
    g_ref[...] = _dot(_sigmoid(xg).astype(BF16), g2_ref[...])

    lane_head = _iota2((GROUP, GROUP), 0) // B_HD == _iota2((GROUP, GROUP), 1) // B_HD
    seg = lane_head.astype(BF16)
    kk = k * kkw_ref[...]
    nrm = jnp.sqrt(_dot3_rhs(kk * kk, seg))
    kk = kk / jnp.maximum(nrm, 1e-12)
    k2 = k * (1.0 + (a - 1.0) * ka_ref[...])
    b = kk * a
    bonus_ref[...] = _dot3_rhs(r * k2 * rk_ref[...], seg) * v

    row = _iota2((tm, tm), 0)
    col = _iota2((tm, tm), 1)
    same = row // B_CHUNK == col // B_CHUNK
    cl = _dot3(jnp.logical_and(same, col <= row).astype(BF16), logd)
    cl_last = _dot3(same.astype(BF16), logd)
    e_neg = jnp.exp(-cl)
    e_end = jnp.exp(cl_last - cl)
    at = -kk * jnp.exp(cl - logd)
    bt = b * e_neg
    kt = k2 * e_neg
    rt = r * jnp.exp(cl)
    bd = b * e_end
    kd = k2 * e_end
    for h in range(B_HEADS):
        cs = slice(h * B_HD, (h + 1) * B_HD)
        at_ref[h] = at[:, cs].astype(BF16)
        bt_ref[h] = bt[:, cs].astype(BF16)
        kt_ref[h] = kt[:, cs].astype(BF16)
        rt_ref[h] = rt[:, cs].astype(BF16)
        v_ref[h] = v[:, cs].astype(BF16)
        bd_ref[h] = bd[:, cs].astype(BF16)
        kd_ref[h] = kd[:, cs].astype(BF16)
    pc = jnp.exp(cl_last)
    for c in range(tm // B_CHUNK):
        pc_ref[c] = pc[c * B_CHUNK:c * B_CHUNK + 1, :]


def _dot3_rhs(x, sel):
    hi, mid, lo = _split3(x)
    return _dot(hi, sel) + _dot(mid, sel) + _dot(lo, sel)


def _rwkv_scan_kernel(at_ref, bt_ref, kt_ref, rt_ref, v_ref, bd_ref, kd_ref, pc_ref, g_ref, bonus_ref,
                      lng_ref, lnb_ref, o_ref, s_ref, y_ref):
    @pl.when(pl.program_id(0) == 0)
    def _():
        s_ref[...] = jnp.zeros_like(s_ref)

    c = B_CHUNK
    row = _iota2((c, c), 0)
    col = _iota2((c, c), 1)
    strict = col < row
    incl = col <= row
    pc = pc_ref[0]
    for h in range(B_HEADS):
        at, bt, kt, rt = at_ref[h], bt_ref[h], kt_ref[h], rt_ref[h]
        v, bd, kd = v_ref[h], bd_ref[h], kd_ref[h]
        st = s_ref[h]
        sb = st.astype(BF16)
        ab = jnp.where(strict, _dot_nt(at, bt), 0.0)
        ak = jnp.where(strict, _dot_nt(at, kt), 0.0)
        rb = jnp.where(incl, _dot_nt(rt, bt), 0.0)
        rk = jnp.where(incl, _dot_nt(rt, kt), 0.0)
        x = _dot_nt(at, sb) + _dot(ak.astype(BF16), v)
        n = ab
        for lvl in range(6):
            nb = n.astype(BF16)
            x = x + _dot(nb, x.astype(BF16))
            if lvl < 5:
                n = _dot(nb, nb)
        ub = x.astype(BF16)
        y = _dot_nt(rt, sb) + _dot(rb.astype(BF16), ub) + _dot(rk.astype(BF16), v)
        cs = slice(h * B_HD, (h + 1) * B_HD)
        s_ref[h] = st * pc[:, cs] + _dot_tn(ub, bd) + _dot_tn(v, kd)
        ym = jnp.mean(y, axis=-1, keepdims=True)
        yc = y - ym
        yv = jnp.mean(yc * yc, axis=-1, keepdims=True)
        y_ref[:, cs] = yc * lax.rsqrt(yv + B_LNX_EPS)
    y = y_ref[...] * lng_ref[...] + lnb_ref[...] + bonus_ref[...]
    o_ref[...] = (y * g_ref[...]).astype(o_ref.dtype)


def _rwkv(pb, mu, w0, w2, a0, a2, g2, k_k, k_a, r_k, lnx_g, lnx_b):
    s = pb.shape[0]
    tm = min(B_TM, s)
    nch = s // B_CHUNK
    vecw = pl.BlockSpec((1, B_W), lambda i: (0, 0))
    vec = pl.BlockSpec((1, GROUP), lambda i: (0, 0))
    lora = pl.BlockSpec((LANES, GROUP), lambda i: (0, 0))
    hm = pl.BlockSpec((B_HEADS, tm, B_HD), lambda i: (0, i, 0))
    full = pl.BlockSpec((tm, GROUP), lambda i: (i, 0))
    hm_shape = jax.ShapeDtypeStruct((B_HEADS, s, B_HD), BF16)
    outs = pl.pallas_call(
        _rwkv_prep_kernel,
        grid=(s // tm,),
        in_specs=[pl.BlockSpec((tm, B_W), lambda i: (i, 0)),
                  pl.BlockSpec((8, B_W), lambda i: (jnp.maximum(i * (tm // 8) - 1, 0), 0)),
                  vecw, vec, lora, vec, lora, lora, vec, vec, vec],
        out_specs=[hm] * 7 + [pl.BlockSpec((tm // B_CHUNK, 1, GROUP), lambda i: (i, 0, 0)), full, full],
        out_shape=[hm_shape] * 7 + [jax.ShapeDtypeStruct((nch, 1, GROUP), F32),
                                    jax.ShapeDtypeStruct((s, GROUP), F32),
                                    jax.ShapeDtypeStruct((s, GROUP), F32)],
        compiler_params=_params("parallel"),
    )(pb, pb, mu, w0.reshape(1, GROUP), w2, a0.reshape(1, GROUP), a2, g2,
      k_k.reshape(1, GROUP), k_a.reshape(1, GROUP), r_k.reshape(1, GROUP))
    at, bt, kt, rt, v, bd, kd, pc, g, bonus = outs

    c = B_CHUNK
    hmc = pl.BlockSpec((B_HEADS, c, B_HD), lambda i: (0, i, 0))
    fullc = pl.BlockSpec((c, GROUP), lambda i: (i, 0))
    return pl.pallas_call(
        _rwkv_scan_kernel,
        grid=(nch,),
        in_specs=[hmc] * 7 + [pl.BlockSpec((1, 1, GROUP), lambda i: (i, 0, 0)), fullc, fullc, vec, vec],
        out_specs=fullc,
        out_shape=jax.ShapeDtypeStruct((s, GROUP), BF16),
        scratch_shapes=[pltpu.VMEM((B_HEADS, B_HD, B_HD), F32), pltpu.VMEM((c, GROUP), F32)],
        compiler_params=_params("arbitrary"),
    )(at, bt, kt, rt, v, bd, kd, pc, g, bonus, lnx_g.reshape(1, GROUP), lnx_b.reshape(1, GROUP))


def _moba_prep_kernel(p_ref, cos_ref, sin_ref, q_ref, k_ref, vt_ref, sel_ref, kmean_ref):
    i = pl.program_id(0)
    nb = kmean_ref.shape[0]

    @pl.when(i == 0)
    def _():
        kmean_ref[...] = jnp.zeros_like(kmean_ref)

    p = p_ref[...]
    cos = cos_ref[...]
    sin = sin_ref[...]
    first_half = (_iota2((C_BLOCK, GROUP), 1) % C_HD) < C_HD // 2

    def rope(x):
        rot = jnp.where(first_half, pltpu.roll(x, GROUP - C_HD // 2, axis=1), pltpu.roll(x, C_HD // 2, axis=1))
        return x * cos + rot * sin

    q = rope(p[:, 0:GROUP]) * (C_HD ** -0.5)
    k = rope(p[:, GROUP:2 * GROUP])
    v = p[:, 2 * GROUP:3 * GROUP]
    vt = v.T
    for h in range(C_HEADS):
        cs = slice(h * C_HD, (h + 1) * C_HD)
        q_ref[h] = q[:, cs].astype(BF16)
        k_ref[h, 0] = k[:, cs].astype(BF16)
        vt_ref[h, 0] = vt[cs, :].astype(BF16)

    km = kmean_ref[...]
    blk_i = _iota2((nb, C_BLOCK), 0)
    blk = blk_i.astype(F32)
    valid = blk_i < i
    lane_h = _iota2((nb, GROUP), 1) // C_HD
    for h in range(C_HEADS):
        sc = _dot3_nt(jnp.where(lane_h == h, km, 0.0), q)
        cur = jnp.where(valid, sc, -jnp.inf)
        sel = jnp.zeros((nb, C_BLOCK), F32)
        for _ in range(C_TOPK):
            m = jnp.max(cur, axis=0, keepdims=True)
            first = jnp.min(jnp.where(cur == m, blk, float(nb)), axis=0, keepdims=True)
            pick = jnp.logical_and(blk == first, m > -jnp.inf)
            sel = jnp.where(pick, 1.0, sel)
            cur = jnp.where(pick, -jnp.inf, cur)
        sel_ref[h] = sel
    kmean_ref[pl.ds(i, 1), :] = jnp.mean(k, axis=0, keepdims=True)


def _moba_attn_kernel(q_ref, k_ref, vt_ref, sel_ref, o_ref):
    i = pl.program_id(1)
    q = q_ref[0]
    st = _dot_nt(k_ref[0, i], q)
    st = jnp.where(_iota2(st.shape, 0) <= _iota2(st.shape, 1), st, NEG)
    m = jnp.max(st, axis=0, keepdims=True)
    p = jnp.exp(st - m)
    l = jnp.sum(p, axis=0, keepdims=True)
    acc = _dot(vt_ref[0, i], p.astype(BF16))

    def body(j, carry):
        m, l, acc = carry
        st = _dot_nt(k_ref[0, j], q)
        st = jnp.where(sel_ref[0, pl.ds(j, 1), :] > 0.0, st, NEG)
        m_new = jnp.maximum(m, jnp.max(st, axis=0, keepdims=True))
        alpha = jnp.exp(m - m_new)
        p = jnp.exp(st - m_new)
        l = alpha * l + jnp.sum(p, axis=0, keepdims=True)
        acc = alpha * acc + _dot(vt_ref[0, j], p.astype(BF16))
        return m_new, l, acc

    m, l, acc = lax.fori_loop(0, i, body, (m, l, acc))
    o_ref[0] = acc / l


def _moba(pc, cos, sin):
    s = pc.shape[0]
    nb = s // C_BLOCK
    tile = pl.BlockSpec((C_BLOCK, GROUP), lambda i: (i, 0))
    q, k, vt, sel = pl.pallas_call(
        _moba_prep_kernel,
        grid=(nb,),
        in_specs=[pl.BlockSpec((C_BLOCK, 3 * GROUP), lambda i: (i, 0)), tile, tile],
        out_specs=[pl.BlockSpec((C_HEADS, C_BLOCK, C_HD), lambda i: (0, i, 0)),
                   pl.BlockSpec((C_HEADS, 1, C_BLOCK, C_HD), lambda i: (0, i, 0, 0)),
                   pl.BlockSpec((C_HEADS, 1, C_HD, C_BLOCK), lambda i: (0, i, 0, 0)),
                   pl.BlockSpec((C_HEADS, nb, C_BLOCK), lambda i: (0, 0, i))],
        out_shape=[jax.ShapeDtypeStruct((C_HEADS, s, C_HD), BF16),
                   jax.ShapeDtypeStruct((C_HEADS, nb, C_BLOCK, C_HD), BF16),
                   jax.ShapeDtypeStruct((C_HEADS, nb, C_HD, C_BLOCK), BF16),
                   jax.ShapeDtypeStruct((C_HEADS, nb, s), F32)],
        scratch_shapes=[pltpu.VMEM((nb, GROUP), F32)],
        compiler_params=_params("arbitrary"),
    )(pc, cos, sin)
    ot = pl.pallas_call(
        _moba_attn_kernel,
        grid=(C_HEADS, nb),
        in_specs=[pl.BlockSpec((1, C_BLOCK, C_HD), lambda h, i: (h, i, 0)),
                  pl.BlockSpec((1, nb, C_BLOCK, C_HD), lambda h, i: (h, 0, 0, 0)),
                  pl.BlockSpec((1, nb, C_HD, C_BLOCK), lambda h, i: (h, 0, 0, 0)),
                  pl.BlockSpec((1, nb, C_BLOCK), lambda h, i: (h, 0, i))],
        out_specs=pl.BlockSpec((1, C_HD, C_BLOCK), lambda h, i: (h, 0, i)),
        out_shape=jax.ShapeDtypeStruct((C_HEADS, C_HD, s), F32),
        compiler_params=_params("parallel", "parallel"),
    )(q, k, vt, sel)
    return ot.reshape(GROUP, s)


def _rope_tables(s):
    half = C_HD // 2
    inv = ROPE_THETA ** (-jnp.arange(half, dtype=F32) / half)
    ang = jnp.arange(s, dtype=F32)[:, None] * inv[None, :]
    cos = jnp.cos(ang)
    sin = jnp.sin(ang)
    cos_h = jnp.concatenate([cos, cos], axis=1)
    sin_h = jnp.concatenate([-sin, sin], axis=1)
    return jnp.tile(cos_h, (1, C_HEADS)), jnp.tile(sin_h, (1, C_HEADS))


D_W = 2 * D_HEADS * D_DK + 2 * GROUP + LANES
D_QK = D_HEADS * D_DK
D_LEVELS = 6


def _gla_kernel(p_ref, w2_ref, gb_ref, ng_ref, o_ref, s_ref):
    @pl.when(pl.program_id(0) == 0)
    def _():
        s_ref[...] = jnp.zeros_like(s_ref)

    c = D_CHUNK
    p = p_ref[...]
    q = p[:, 0:D_QK] * (D_DK ** -0.5)
    k = p[:, D_QK:2 * D_QK]
    v = p[:, 2 * D_QK:2 * D_QK + GROUP]
    og = p[:, 2 * D_QK + GROUP:2 * D_QK + 2 * GROUP]
    xg = p[:, 2 * D_QK + 2 * GROUP:]
    gl = -_softplus(-(_dot(xg.astype(BF16), w2_ref[...]) + gb_ref[...])) / D_GATE_TEMP
    row = _iota2((c, c), 0)
    col = _iota2((c, c), 1)
    cum = _dot3((col <= row).astype(BF16), gl)
    last = cum[c - 1:c, :]

    lane_head = _iota2((c, D_QK), 1) // D_DK
    row4 = _iota2((D_HEADS * c, c), 0) % c
    col4 = _iota2((D_HEADS * c, c), 1)
    scores = jnp.zeros((D_HEADS * c, c), F32)
    for lvl in range(D_LEVELS + 1):
        if lvl < D_LEVELS:
            b = c >> lvl
            half = b // 2
            pick = (col == (row // b) * b + (half - 1)).astype(BF16)
            rho = _dot3(pick, cum)
            qs = q * jnp.exp(jnp.minimum(cum - rho, 0.0))
            ks = k * jnp.exp(jnp.minimum(rho - cum, 0.0))
            mask = jnp.logical_and(row4 // b == col4 // b,
                                   jnp.logical_and(row4 % b >= half, col4 % b < half))
        else:
            qs, ks = q, k
            mask = row4 == col4
        qm = jnp.concatenate([jnp.where(lane_head == h, qs, 0.0) for h in range(D_HEADS)], axis=0)
        scores = scores + jnp.where(mask, _dot_nt(qm.astype(BF16), ks.astype(BF16)), 0.0)

    qe = (q * jnp.exp(cum)).astype(BF16)
    kd = k * jnp.exp(last - cum)
    e_last = jnp.exp(last)
    for h in range(D_HEADS):
        vs = slice(h * D_DV, (h + 1) * D_DV)
        vh = v[:, vs].astype(BF16)
        st = s_ref[h]
        o = _dot(scores[h * c:(h + 1) * c, :].astype(BF16), vh) + _dot_nt(qe, st.astype(BF16))
        kdm = jnp.where(lane_head == h, kd, 0.0)
        s_ref[h] = st * e_last + _dot_tn(vh, kdm.astype(BF16))
        on = o * lax.rsqrt(jnp.mean(o * o, axis=-1, keepdims=True) + EPS) * ng_ref[...]
        ogh = og[:, vs]
        o_ref[:, vs] = (on * (ogh * _sigmoid(ogh))).astype(o_ref.dtype)


def _gla(pd, w2, gate_b, norm_g):
    s = pd.shape[0]
    c = D_CHUNK
    return pl.pallas_call(
        _gla_kernel,
        grid=(s // c,),
        in_specs=[pl.BlockSpec((c, D_W), lambda i: (i, 0)),
                  pl.BlockSpec((LANES, D_QK), lambda i: (0, 0)),
                  pl.BlockSpec((1, D_QK), lambda i: (0, 0)),
                  pl.BlockSpec((1, D_DV), lambda i: (0, 0))],
        out_specs=pl.BlockSpec((c, GROUP), lambda i: (i, 0)),
        out_shape=jax.ShapeDtypeStruct((s, GROUP), BF16),
        scratch_shapes=[pltpu.VMEM((D_HEADS, D_DV, D_QK), F32)],
        compiler_params=_params("arbitrary"),
    )(pd, w2, gate_b.reshape(1, D_QK), norm_g.reshape(1, D_DV))


def _ffn_up_kernel(h_ref, wg_ref, wu_ref, cwg_ref, cwu_ref, cbg_ref, cbu_ref, o_ref, gbuf, ubuf):
    tm = h_ref.shape[0]

    @pl.when(pl.program_id(1) == 0)
    def _():
        gbuf[0:8, :] = jnp.zeros((8, gbuf.shape[1]), F32)
        ubuf[0:8, :] = jnp.zeros((8, ubuf.shape[1]), F32)

    def conv(buf, w_ref, cw_ref, cb_ref):
        buf[8:8 + tm, :] = _dot(h_ref[...], w_ref[...])
        acc = (cb_ref[...] + buf[6:6 + tm, :] * cw_ref[0:1, :] + buf[7:7 + tm, :] * cw_ref[1:2, :]
               + buf[8:8 + tm, :] * cw_ref[2:3, :])
        buf[0:8, :] = buf[tm:tm + 8, :]
        return acc

    gate = conv(gbuf, wg_ref, cwg_ref, cbg_ref)
    up = conv(ubuf, wu_ref, cwu_ref, cbu_ref)
    o_ref[...] = (gate * _sigmoid(gate) * up).astype(o_ref.dtype)


def _ffn_up(h, w_up, conv_w, conv_b, tm=512, tn=512):
    m, k = h.shape
    nj = D_FF // tn
    cb = conv_b.reshape(1, 2 * D_FF)
    return pl.pallas_call(
        _ffn_up_kernel,
        grid=(nj, m // tm),
        in_specs=[pl.BlockSpec((tm, k), lambda j, i: (i, 0)),
                  pl.BlockSpec((k, tn), lambda j, i: (0, j)),
                  pl.BlockSpec((k, tn), lambda j, i: (0, j + nj)),
                  pl.BlockSpec((3, tn), lambda j, i: (0, j)),
                  pl.BlockSpec((3, tn), lambda j, i: (0, j + nj)),
                  pl.BlockSpec((1, tn), lambda j, i: (0, j)),
                  pl.BlockSpec((1, tn), lambda j, i: (0, j + nj))],
        out_specs=pl.BlockSpec((tm, tn), lambda j, i: (i, j)),
        out_shape=jax.ShapeDtypeStruct((m, D_FF), BF16),
        scratch_shapes=[pltpu.VMEM((tm + 8, tn), F32), pltpu.VMEM((tm + 8, tn), F32)],
        compiler_params=_params("parallel", "arbitrary"),
    )(h, w_up, w_up, conv_w, conv_w, cb, cb)


def _pad_cols(w, width):
    return jnp.pad(w, ((0, 0), (0, width - w.shape[1])))


def _pad_rows(w, height):
    return jnp.pad(w, ((0, height - w.shape[0]), (0, 0)))


def _split_w_in(w):
    a_in = 2 * GROUP
    b0 = a_in
    wa = w[:, :a_in]
    rkv = w[:, b0:b0 + 3 * GROUP]
    xw = w[:, b0 + 3 * GROUP:b0 + 3 * GROUP + 32]
    xa = w[:, b0 + 3 * GROUP + 32:b0 + 3 * GROUP + 64]
    xg = w[:, b0 + 3 * GROUP + 64:b0 + 3 * GROUP + 160]
    wb = jnp.concatenate([rkv, _pad_cols(xw, LANES), _pad_cols(xa, LANES), _pad_cols(xg, LANES)], axis=1)
    c0 = b0 + 3 * GROUP + 160
    wc = w[:, c0:c0 + 3 * GROUP]
    d0 = c0 + 3 * GROUP
    qkv = w[:, d0:d0 + 2 * D_QK + GROUP]
    dxg = w[:, d0 + 2 * D_QK + GROUP:d0 + 2 * D_QK + GROUP + 16]
    og = w[:, d0 + 2 * D_QK + GROUP + 16:d0 + 2 * D_QK + 2 * GROUP + 16]
    wd = jnp.concatenate([qkv, og, _pad_cols(dxg, LANES)], axis=1)
    return [t.astype(BF16) for t in (wa, wb, wc, wd)]


def _split_mu(mu):
    rkv = mu[:3 * GROUP]
    pads = [jnp.pad(mu[3 * GROUP + lo:3 * GROUP + hi], (0, LANES - (hi - lo))) for lo, hi in ((0, 32), (32, 64), (64, 160))]
    return jnp.concatenate([rkv] + pads).reshape(1, B_W)


def kernel(x, mix_norm_g, w_in, a_ln_g, a_ln_b, a_ws, a_bs, b_mu, b_w0, b_w2, b_a0, b_a2, b_g2, b_kk, b_ka, b_rk, b_lnx_g, b_lnx_b, d_gate_w2, d_gate_b, d_norm_g, w_out, ffn_norm_g, w_up, conv_w, conv_b, w_down, final_norm_g):
    bsz, s, d = x.shape
    depth = w_in.shape[0]
    cos, sin = _rope_tables(s)
    outs = []
    for bi in range(bsz):
        xc = x[bi]
        for l in range(depth):
            h = _rmsnorm(xc, mix_norm_g[l], BF16)
            wa, wb, wc, wd = _split_w_in(w_in[l])
            pa = _matmul(h, wa, F32)
            pb = _matmul(h, wb, F32)
            pc = _matmul(h, wc, F32)
            pd = _matmul(h, wd, F32)
            ya = _gmlp(pa, a_ln_g[l], a_ln_b[l], a_ws[l], a_bs[l])
            yb = _rwkv(pb, _split_mu(b_mu[l]), b_w0[l], _pad_rows(b_w2[l], LANES).astype(BF16), b_a0[l],
                       _pad_rows(b_a2[l], LANES).astype(BF16), _pad_rows(b_g2[l], LANES).astype(BF16),
                       b_kk[l], b_ka[l], b_rk[l], b_lnx_g[l], b_lnx_b[l])
            yct = _moba(pc, cos, sin)
            yd = _gla(pd, _pad_rows(d_gate_w2[l], LANES).astype(BF16), d_gate_b[l], d_norm_g[l])
            xc = _outproj(ya, yb, yct, yd, w_out[l].astype(BF16), xc)
            h2 = _rmsnorm(xc, ffn_norm_g[l], BF16)
            act = _ffn_up(h2, w_up[l].astype(BF16), conv_w[l], conv_b[l])
            xc = _matmul_res(act, w_down[l].astype(BF16), xc)
        outs.append(_rmsnorm(xc, final_norm_g, F32))
    return jnp.stack(outs, axis=0)
```

```python
import functools
import math

import jax
import jax.numpy as jnp
import numpy as np
from jax import lax
from jax.experimental import pallas as pl
from jax.experimental.pallas import tpu as pltpu

F32 = jnp.float32
BF16 = jnp.bfloat16

EPS = 1e-6
GROUP = 512
LANES = 128
VMEM_LIMIT = 56 * 1024 * 1024

A_HEADS, A_CH, A_CHUNK = 4, 128, 128
B_HEADS, B_HD, B_CHUNK = 8, 64, 64
B_LNX_EPS = 64e-5
C_HEADS, C_HD, C_BLOCK, C_TOPK = 8, 64, 256, 3
ROPE_THETA = 10000.0
D_HEADS, D_DK, D_DV, D_CHUNK = 4, 64, 128, 64
D_GATE_TEMP = 16.0
D_FF = 5632
NEG = -1e30
LOG2E = 1.4426950408889634


def _params(*sem):
    return pltpu.CompilerParams(dimension_semantics=sem, vmem_limit_bytes=VMEM_LIMIT)


def _dot(a, b):
    return jnp.dot(a, b, preferred_element_type=F32)


def _dot_nt(a, b):
    return lax.dot_general(a, b, (((1,), (1,)), ((), ())), preferred_element_type=F32)


def _dot_tn(a, b):
    return lax.dot_general(a, b, (((0,), (0,)), ((), ())), preferred_element_type=F32)


def _split3(x):
    hi = x.astype(BF16)
    r1 = x - hi.astype(F32)
    mid = r1.astype(BF16)
    lo = (r1 - mid.astype(F32)).astype(BF16)
    return hi, mid, lo


def _dot3(sel, x):
    hi, mid, lo = _split3(x)
    return _dot(sel, hi) + _dot(sel, mid) + _dot(sel, lo)


def _dot3_nt(a, b):
    ah, am, al = _split3(a)
    bh, bm, bl = _split3(b)
    return (_dot_nt(ah, bh) + (_dot_nt(ah, bm) + _dot_nt(am, bh))
            + (_dot_nt(am, bm) + _dot_nt(ah, bl) + _dot_nt(al, bh)))


def _sigmoid(x):
    return 1.0 / (1.0 + jnp.exp(-x))


def _softplus(x):
    return jnp.maximum(x, 0.0) + jnp.log(1.0 + jnp.exp(-jnp.abs(x)))


def _iota2(shape, axis):
    return lax.broadcasted_iota(jnp.int32, shape, axis)


def _rmsnorm_kernel(x_ref, g_ref, o_ref):
    x = x_ref[...]
    ms = jnp.mean(x * x, axis=-1, keepdims=True)
    o_ref[...] = (x * lax.rsqrt(ms + EPS) * g_ref[...]).astype(o_ref.dtype)


def _rmsnorm(x, g, out_dtype, tm=512):
    m, d = x.shape
    return pl.pallas_call(
        _rmsnorm_kernel,
        name="rmsnorm",
        grid=(m // tm,),
        in_specs=[pl.BlockSpec((tm, d), lambda i: (i, 0)),
                  pl.BlockSpec((1, d), lambda i: (0, 0))],
        out_specs=pl.BlockSpec((tm, d), lambda i: (i, 0)),
        out_shape=jax.ShapeDtypeStruct((m, d), out_dtype),
        compiler_params=_params("parallel"),
    )(x, g.reshape(1, d))


def _mm_kernel(x_ref, w_ref, o_ref):
    o_ref[...] = _dot(x_ref[...], w_ref[...]).astype(o_ref.dtype)


def _matmul(x, w, out_dtype, tm=512):
    m, k = x.shape
    n = w.shape[1]
    return pl.pallas_call(
        _mm_kernel,
        name="inproj",
        grid=(m // tm,),
        in_specs=[pl.BlockSpec((tm, k), lambda i: (i, 0)),
                  pl.BlockSpec((k, n), lambda i: (0, 0))],
        out_specs=pl.BlockSpec((tm, n), lambda i: (i, 0)),
        out_shape=jax.ShapeDtypeStruct((m, n), out_dtype),
        compiler_params=_params("parallel"),
    )(x, w)


def _mm_res_kernel(x_ref, w_ref, r_ref, o_ref):
    o_ref[...] = r_ref[...] + _dot(x_ref[...], w_ref[...])


def _matmul_res(x, w, res, tm=512, tn=512):
    m, k = x.shape
    n = w.shape[1]
    return pl.pallas_call(
        _mm_res_kernel,
        name="ffn_down",
        grid=(n // tn, m // tm),
        in_specs=[pl.BlockSpec((tm, k), lambda j, i: (i, 0)),
                  pl.BlockSpec((k, tn), lambda j, i: (0, j)),
                  pl.BlockSpec((tm, tn), lambda j, i: (i, j))],
        out_specs=pl.BlockSpec((tm, tn), lambda j, i: (i, j)),
        out_shape=jax.ShapeDtypeStruct((m, n), F32),
        compiler_params=_params("parallel", "parallel"),
    )(x, w, res)


def _outproj_kernel(ya_ref, yb_ref, yct_ref, yd_ref, w_ref, x_ref, o_ref):
    yc = yct_ref[...].T.astype(BF16)
    acc = _dot(ya_ref[...], w_ref[0:GROUP, :])
    acc += _dot(yb_ref[...], w_ref[GROUP:2 * GROUP, :])
    acc += _dot(yc, w_ref[2 * GROUP:3 * GROUP, :])
    acc += _dot(yd_ref[...], w_ref[3 * GROUP:4 * GROUP, :])
    o_ref[...] = x_ref[...] + acc


def _outproj(ya, yb, yct, yd, w, x, tm=512, tn=1024):
    m = x.shape[0]
    n = w.shape[1]
    row = pl.BlockSpec((tm, GROUP), lambda j, i: (i, 0))
    return pl.pallas_call(
        _outproj_kernel,
        name="outproj",
        grid=(n // tn, m // tm),
        in_specs=[row, row,
                  pl.BlockSpec((GROUP, tm), lambda j, i: (0, i)),
                  row,
                  pl.BlockSpec((4 * GROUP, tn), lambda j, i: (0, j)),
                  pl.BlockSpec((tm, tn), lambda j, i: (i, j))],
        out_specs=pl.BlockSpec((tm, tn), lambda j, i: (i, j)),
        out_shape=jax.ShapeDtypeStruct((m, n), F32),
        compiler_params=_params("parallel", "parallel"),
    )(ya, yb, yct, yd, w, x)


def _gmlp_kernel(p_ref, lng_ref, lnb_ref, ws_ref, bsb_ref, o_ref, *, nchunk):
    p = p_ref[...]
    z = 0.5 * p * (1.0 + jnp.tanh(math.sqrt(2.0 / math.pi) * (p + 0.044715 * (p * p * p))))
    u = z[:, :GROUP]
    v = z[:, GROUP:]
    mu = jnp.mean(v, axis=-1, keepdims=True)
    vc = v - mu
    var = jnp.mean(vc * vc, axis=-1, keepdims=True)
    vn = (vc * lax.rsqrt(var + EPS) * lng_ref[...] + lnb_ref[...]).astype(BF16)
    tri = _iota2((A_CHUNK, A_CHUNK), 1) <= _iota2((A_CHUNK, A_CHUNK), 0)
    for h in range(A_HEADS):
        w = jnp.where(tri, ws_ref[h], 0.0).astype(BF16)
        cs = slice(h * A_CH, (h + 1) * A_CH)
        for c in range(nchunk):
            rs = slice(c * A_CHUNK, (c + 1) * A_CHUNK)
            mixed = _dot(w, vn[rs, cs]) + bsb_ref[:, cs]
            o_ref[rs, cs] = (u[rs, cs] * mixed).astype(o_ref.dtype)


def _gmlp(pa, ln_g, ln_b, ws, bs, nchunk=4):
    s = pa.shape[0]
    tm = nchunk * A_CHUNK
    bsb = jnp.repeat(bs.T, A_CH, axis=1)
    vec = pl.BlockSpec((1, GROUP), lambda i: (0, 0))
    return pl.pallas_call(
        functools.partial(_gmlp_kernel, nchunk=nchunk),
        name="gmlp",
        grid=(s // tm,),
        in_specs=[pl.BlockSpec((tm, 2 * GROUP), lambda i: (i, 0)), vec, vec,
                  pl.BlockSpec((A_HEADS, A_CHUNK, A_CHUNK), lambda i: (0, 0, 0)),
                  pl.BlockSpec((A_CHUNK, GROUP), lambda i: (0, 0))],
        out_specs=pl.BlockSpec((tm, GROUP), lambda i: (i, 0)),
        out_shape=jax.ShapeDtypeStruct((s, GROUP), BF16),
        compiler_params=_params("parallel"),
    )(pa, ln_g.reshape(1, GROUP), ln_b.reshape(1, GROUP), ws, bsb)


B_W = 3 * GROUP + 3 * LANES
B_TM = 512
B_STATE_CHUNKS = 4


def _rwkv_prep_kernel(p_ref, pprev_ref, mu_ref, w0_ref, w2_ref, a0_ref, a2_ref, g2_ref,
                      kkw_ref, ka_ref, rk_ref,
                      at_ref, bt_ref, kt_ref, rt_ref, v_ref, bd_ref, kd_ref, pc_ref, g_ref, bonus_ref):
    i = pl.program_id(0)
    p = p_ref[...]
    tm = p.shape[0]
    last = jnp.where(i == 0, 0.0, pprev_ref[7:8, :])
    prev = jnp.where(_iota2(p.shape, 0) == 0, last, pltpu.roll(p, 1, axis=0))
    xs = p + (prev - p) * mu_ref[...]
    r = xs[:, 0:GROUP]
    k = xs[:, GROUP:2 * GROUP]
    v = xs[:, 2 * GROUP:3 * GROUP]
    xw = xs[:, 3 * GROUP:3 * GROUP + LANES]
    xa = xs[:, 3 * GROUP + LANES:3 * GROUP + 2 * LANES]
    xg = xs[:, 3 * GROUP + 2 * LANES:3 * GROUP + 3 * LANES]
    w = -_softplus(-(w0_ref[...] + _dot(jnp.tanh(xw).astype(BF16), w2_ref[...]))) - 0.5
    logd = -jnp.exp(w)
    a = _sigmoid(a0_ref[...] + _dot(xa.astype(BF16), a2_ref[...]))
    g_ref[...] = _dot(_sigmoid(xg).astype(BF16), g2_ref[...])

    lane_head = _iota2((GROUP, GROUP), 0) // B_HD == _iota2((GROUP, GROUP), 1) // B_HD
    seg = lane_head.astype(BF16)
    kk = k * kkw_ref[...]
    nrm = jnp.sqrt(_dot3_rhs(kk * kk, seg))
    kk = kk / jnp.maximum(nrm, 1e-12)
    k2 = k * (1.0 + (a - 1.0) * ka_ref[...])
    b = kk * a
    bonus_ref[...] = _dot3_rhs(r * k2 * rk_ref[...], seg) * v

    row = _iota2((tm, tm), 0)
    col = _iota2((tm, tm), 1)
    same = row // B_CHUNK == col // B_CHUNK
    cl = _dot3(jnp.logical_and(same, col <= row).astype(BF16), logd)
    cl_last = _dot3(same.astype(BF16), logd)
    e_neg = jnp.exp(-cl)
    e_end = jnp.exp(cl_last - cl)
    at = -kk * jnp.exp(cl - logd)
    bt = b * e_neg
    kt = k2 * e_neg
    rt = r * jnp.exp(cl)
    bd = b * e_end
    kd = k2 * e_end
    for h in range(B_HEADS):
        cs = slice(h * B_HD, (h + 1) * B_HD)
        at_ref[h] = at[:, cs].astype(BF16)
        bt_ref[h] = bt[:, cs].astype(BF16)
        kt_ref[h] = kt[:, cs].astype(BF16)
        rt_ref[h] = rt[:, cs].astype(BF16)
        v_ref[h] = v[:, cs].astype(BF16)
        bd_ref[h] = bd[:, cs].astype(BF16)
        kd_ref[h] = kd[:, cs].astype(BF16)
    pc = jnp.exp(cl_last)
    for c in range(tm // B_CHUNK):
        pc_ref[c] = pc[c * B_CHUNK:c * B_CHUNK + 1, :]


def _dot3_rhs(x, sel):
    hi, mid, lo = _split3(x)
    return _dot(hi, sel) + _dot(mid, sel) + _dot(lo, sel)


def _rwkv_chunk_kernel(at_ref, bt_ref, kt_ref, rt_ref, v_ref, bd_ref, kd_ref, r2_ref, y0_ref, m2_ref, sa_ref):
    c = B_CHUNK
    row = _iota2((2 * c, 2 * c), 0)
    col = _iota2((2 * c, 2 * c), 1) % c
    keep = jnp.logical_or(col < row % c, jnp.logical_and(row >= c, col == row % c))
    zeros = jnp.zeros((c, B_HD), BF16)
    heads = range(B_HEADS)
    big = [jnp.where(keep, _dot_nt(jnp.concatenate([at_ref[h], rt_ref[h]], axis=0),
                                   jnp.concatenate([bt_ref[h], kt_ref[h]], axis=0)), 0.0)
           for h in heads]
    n = [big[h][:c, :c] for h in heads]
    akv = [_dot(big[h][:c, c:].astype(BF16), v_ref[h]) for h in heads]
    x = [jnp.concatenate([at_ref[h].astype(F32), akv[h]], axis=1) for h in heads]
    for lvl in range(6):
        nb = [n[h].astype(BF16) for h in heads]
        x = [x[h] + _dot(nb[h], x[h].astype(BF16)) for h in heads]
        if lvl < 5:
            n = [_dot(nb[h], nb[h]) for h in heads]
    xb = [x[h].astype(BF16) for h in heads]
    ry = [_dot(big[h][c:, :].astype(BF16),
               jnp.concatenate([xb[h], jnp.concatenate([zeros, v_ref[h]], axis=1)], axis=0))
          for h in heads]
    ms = [_dot_tn(xb[h], bd_ref[h]) for h in heads]
    vk = [_dot_tn(v_ref[h], kd_ref[h]) for h in heads]
    for h in heads:
        r2_ref[h] = (rt_ref[h].astype(F32) + ry[h][:, :B_HD]).astype(BF16)
        y0_ref[h] = ry[h][:, B_HD:]
        m2_ref[0, h] = ms[h][:B_HD, :].astype(BF16)
        sa_ref[0, h] = ms[h][B_HD:, :] + vk[h]


def _rwkv_state_kernel(r2_ref, y0_ref, m2_ref, sa_ref, pc_ref, g_ref, bonus_ref, lng_ref, lnb_ref,
                       o_ref, s_ref, y_ref, *, nchunk):
    @pl.when(pl.program_id(0) == 0)
    def _():
        s_ref[...] = jnp.zeros_like(s_ref)

    c = B_CHUNK
    for j in range(nchunk):
        rs = slice(j * c, (j + 1) * c)
        pc = pc_ref[j]
        heads = range(B_HEADS)
        st = [s_ref[h] for h in heads]
        sb = [st[h].astype(BF16) for h in heads]
        sm = [_dot(sb[h], m2_ref[j, h]) for h in heads]
        ys = [_dot_nt(r2_ref[h, rs, :], sb[h]) for h in heads]
        for h in heads:
            cs = slice(h * B_HD, (h + 1) * B_HD)
            s_ref[h] = st[h] * pc[:, cs] + sm[h] + sa_ref[j, h]
            y = ys[h] + y0_ref[h, rs, :]
            ym = jnp.mean(y, axis=-1, keepdims=True)
            yc = y - ym
            yv = jnp.mean(yc * yc, axis=-1, keepdims=True)
            y_ref[rs, cs] = yc * lax.rsqrt(yv + B_LNX_EPS)
    y = y_ref[...] * lng_ref[...] + lnb_ref[...] + bonus_ref[...]
    o_ref[...] = (y * g_ref[...]).astype(o_ref.dtype)


def _rwkv(pb, mu, w0, w2, a0, a2, g2, k_k, k_a, r_k, lnx_g, lnx_b):
    s = pb.shape[0]
    tm = min(B_TM, s)
    nch = s // B_CHUNK
    vecw = pl.BlockSpec((1, B_W), lambda i: (0, 0))
    vec = pl.BlockSpec((1, GROUP), lambda i: (0, 0))
    lora = pl.BlockSpec((LANES, GROUP), lambda i: (0, 0))
    hm = pl.BlockSpec((B_HEADS, tm, B_HD), lambda i: (0, i, 0))
    full = pl.BlockSpec((tm, GROUP), lambda i: (i, 0))
    hm_shape = jax.ShapeDtypeStruct((B_HEADS, s, B_HD), BF16)
    outs = pl.pallas_call(
        _rwkv_prep_kernel,
        name="rwkv_prep",
        grid=(s // tm,),
        in_specs=[pl.BlockSpec((tm, B_W), lambda i: (i, 0)),
                  pl.BlockSpec((8, B_W), lambda i: (jnp.maximum(i * (tm // 8) - 1, 0), 0)),
                  vecw, vec, lora, vec, lora, lora, vec, vec, vec],
        out_specs=[hm] * 7 + [pl.BlockSpec((tm // B_CHUNK, 1, GROUP), lambda i: (i, 0, 0)), full, full],
        out_shape=[hm_shape] * 7 + [jax.ShapeDtypeStruct((nch, 1, GROUP), F32),
                                    jax.ShapeDtypeStruct((s, GROUP), F32),
                                    jax.ShapeDtypeStruct((s, GROUP), F32)],
        compiler_params=_params("parallel"),
    )(pb, pb, mu, w0.reshape(1, GROUP), w2, a0.reshape(1, GROUP), a2, g2,
      k_k.reshape(1, GROUP), k_a.reshape(1, GROUP), r_k.reshape(1, GROUP))
    at, bt, kt, rt, v, bd, kd, pc, g, bonus = outs

    c = B_CHUNK
    hmc = pl.BlockSpec((B_HEADS, c, B_HD), lambda i: (0, i, 0))
    sq = pl.BlockSpec((1, B_HEADS, B_HD, B_HD), lambda i: (i, 0, 0, 0))
    r2, y0, m2, sa = pl.pallas_call(
        _rwkv_chunk_kernel,
        name="rwkv_chunk",
        grid=(nch,),
        in_specs=[hmc] * 7,
        out_specs=[hmc, hmc, sq, sq],
        out_shape=[hm_shape, jax.ShapeDtypeStruct((B_HEADS, s, B_HD), F32),
                   jax.ShapeDtypeStruct((nch, B_HEADS, B_HD, B_HD), BF16),
                   jax.ShapeDtypeStruct((nch, B_HEADS, B_HD, B_HD), F32)],
        compiler_params=_params("parallel"),
    )(at, bt, kt, rt, v, bd, kd)

    nck = min(B_STATE_CHUNKS, nch)
    rows = nck * c
    hmr = pl.BlockSpec((B_HEADS, rows, B_HD), lambda i: (0, i, 0))
    sqr = pl.BlockSpec((nck, B_HEADS, B_HD, B_HD), lambda i: (i, 0, 0, 0))
    fullr = pl.BlockSpec((rows, GROUP), lambda i: (i, 0))
    return pl.pallas_call(
        functools.partial(_rwkv_state_kernel, nchunk=nck),
        name="rwkv_state",
        grid=(nch // nck,),
        in_specs=[hmr, hmr, sqr, sqr, pl.BlockSpec((nck, 1, GROUP), lambda i: (i, 0, 0)), fullr, fullr, vec, vec],
        out_specs=fullr,
        out_shape=jax.ShapeDtypeStruct((s, GROUP), BF16),
        scratch_shapes=[pltpu.VMEM((B_HEADS, B_HD, B_HD), F32), pltpu.VMEM((rows, GROUP), F32)],
        compiler_params=_params("arbitrary"),
    )(r2, y0, m2, sa, pc, g, bonus, lnx_g.reshape(1, GROUP), lnx_b.reshape(1, GROUP))


def _moba_prep_kernel(p_ref, cos_ref, sin_ref, q_ref, k_ref, vt_ref, bias_ref, kmean_ref):
    i = pl.program_id(0)
    nb = kmean_ref.shape[0]

    @pl.when(i == 0)
    def _():
        kmean_ref[...] = jnp.zeros_like(kmean_ref)

    p = p_ref[...]
    cos = cos_ref[...]
    sin = sin_ref[...]
    first_half = (_iota2((C_BLOCK, GROUP), 1) % C_HD) < C_HD // 2

    def rope(x):
        rot = jnp.where(first_half, pltpu.roll(x, GROUP - C_HD // 2, axis=1), pltpu.roll(x, C_HD // 2, axis=1))
        return x * cos + rot * sin

    q = rope(p[:, 0:GROUP]) * (C_HD ** -0.5)
    k = rope(p[:, GROUP:2 * GROUP])
    v = p[:, 2 * GROUP:3 * GROUP]
    vt = v.T
    for h in range(C_HEADS):
        cs = slice(h * C_HD, (h + 1) * C_HD)
        q_ref[h] = (q[:, cs] * LOG2E).astype(BF16)
        k_ref[h, 0] = k[:, cs].astype(BF16)
        vt_ref[h, 0] = vt[cs, :].astype(BF16)

    km = kmean_ref[...]
    blk_i = _iota2((nb, C_BLOCK), 0)
    blk = blk_i.astype(F32)
    valid = blk_i < i
    lane_h = _iota2((nb, GROUP), 1) // C_HD
    for h in range(C_HEADS):
        sc = _dot3_nt(jnp.where(lane_h == h, km, 0.0), q)
        cur = jnp.where(valid, sc, -jnp.inf)
        bias = jnp.full((nb, C_BLOCK), NEG, F32)
        for _ in range(C_TOPK):
            m = jnp.max(cur, axis=0, keepdims=True)
            first = jnp.min(jnp.where(cur == m, blk, float(nb)), axis=0, keepdims=True)
            pick = jnp.logical_and(blk == first, m > -jnp.inf)
            bias = jnp.where(pick, 0.0, bias)
            cur = jnp.where(pick, -jnp.inf, cur)
        bias_ref[h] = bias
    kmean_ref[pl.ds(i, 1), :] = jnp.mean(k, axis=0, keepdims=True)


C_UNROLL = 4


def _moba_attn_kernel(q_ref, k_ref, vt_ref, bias_ref, o_ref, s_buf):
    i = pl.program_id(1)
    nb = k_ref.shape[1]
    q = q_ref[0]
    own = nb + C_UNROLL - 1
    st = _dot_nt(k_ref[0, i], q)
    st = jnp.where(_iota2(st.shape, 0) <= _iota2(st.shape, 1), st, NEG)
    s_buf[own] = st
    m = jnp.max(st, axis=0, keepdims=True)
    ngroups = (i + C_UNROLL - 1) // C_UNROLL

    def scores(t, m):
        js = [t * C_UNROLL + u for u in range(C_UNROLL)]
        jcs = [jnp.minimum(j, nb - 1) for j in js]
        sts = [_dot_nt(k_ref[0, jc], q) for jc in jcs]
        for j, jc, st in zip(js, jcs, sts):
            st = st + jnp.where(j < i, bias_ref[0, pl.ds(jc, 1), :], NEG)
            s_buf[j] = st
            m = jnp.maximum(m, jnp.max(st, axis=0, keepdims=True))
        return m

    m = lax.fori_loop(0, ngroups, scores, m)

    p = jnp.exp2(s_buf[own] - m)
    l = jnp.sum(p, axis=0, keepdims=True)
    acc = _dot(vt_ref[0, i], p.astype(BF16))

    def values(t, carry):
        l, acc = carry
        js = [t * C_UNROLL + u for u in range(C_UNROLL)]
        ps = [jnp.exp2(s_buf[j] - m) for j in js]
        pv = [_dot(vt_ref[0, jnp.minimum(j, nb - 1)], p.astype(BF16)) for j, p in zip(js, ps)]
        for p, o in zip(ps, pv):
            l = l + jnp.sum(p, axis=0, keepdims=True)
            acc = acc + o
        return l, acc

    l, acc = lax.fori_loop(0, ngroups, values, (l, acc))
    o_ref[0] = acc / l


def _moba(pc, cos, sin):
    s = pc.shape[0]
    nb = s // C_BLOCK
    tile = pl.BlockSpec((C_BLOCK, GROUP), lambda i: (i, 0))
    q, k, vt, bias = pl.pallas_call(
        _moba_prep_kernel,
        name="moba_prep",
        grid=(nb,),
        in_specs=[pl.BlockSpec((C_BLOCK, 3 * GROUP), lambda i: (i, 0)), tile, tile],
        out_specs=[pl.BlockSpec((C_HEADS, C_BLOCK, C_HD), lambda i: (0, i, 0)),
                   pl.BlockSpec((C_HEADS, 1, C_BLOCK, C_HD), lambda i: (0, i, 0, 0)),
                   pl.BlockSpec((C_HEADS, 1, C_HD, C_BLOCK), lambda i: (0, i, 0, 0)),
                   pl.BlockSpec((C_HEADS, nb, C_BLOCK), lambda i: (0, 0, i))],
        out_shape=[jax.ShapeDtypeStruct((C_HEADS, s, C_HD), BF16),
                   jax.ShapeDtypeStruct((C_HEADS, nb, C_BLOCK, C_HD), BF16),
                   jax.ShapeDtypeStruct((C_HEADS, nb, C_HD, C_BLOCK), BF16),
                   jax.ShapeDtypeStruct((C_HEADS, nb, s), F32)],
        scratch_shapes=[pltpu.VMEM((nb, GROUP), F32)],
        compiler_params=_params("arbitrary"),
    )(pc, cos, sin)
    ot = pl.pallas_call(
        _moba_attn_kernel,
        grid=(C_HEADS, nb),
        in_specs=[pl.BlockSpec((1, C_BLOCK, C_HD), lambda h, i: (h, i, 0)),
                  pl.BlockSpec((1, nb, C_BLOCK, C_HD), lambda h, i: (h, 0, 0, 0)),
                  pl.BlockSpec((1, nb, C_HD, C_BLOCK), lambda h, i: (h, 0, 0, 0)),
                  pl.BlockSpec((1, nb, C_BLOCK), lambda h, i: (h, 0, i))],
        out_specs=pl.BlockSpec((1, C_HD, C_BLOCK), lambda h, i: (h, 0, i)),
        out_shape=jax.ShapeDtypeStruct((C_HEADS, C_HD, s), F32),
        scratch_shapes=[pltpu.VMEM((nb + C_UNROLL, C_BLOCK, C_BLOCK), F32)],
        compiler_params=_params("parallel", "parallel"),
        name="moba_attn",
    )(q, k, vt, bias)
    return ot.reshape(GROUP, s)


def _rope_tables(s):
    half = C_HD // 2
    inv = ROPE_THETA ** (-jnp.arange(half, dtype=F32) / half)
    ang = jnp.arange(s, dtype=F32)[:, None] * inv[None, :]
    cos = jnp.cos(ang)
    sin = jnp.sin(ang)
    cos_h = jnp.concatenate([cos, cos], axis=1)
    sin_h = jnp.concatenate([-sin, sin], axis=1)
    return jnp.tile(cos_h, (1, C_HEADS)), jnp.tile(sin_h, (1, C_HEADS))


D_W = 2 * D_HEADS * D_DK + 2 * GROUP + LANES
D_QK = D_HEADS * D_DK
D_LEVELS = 6


def _gla_kernel(p_ref, w2_ref, gb_ref, ng_ref, o_ref, s_ref):
    @pl.when(pl.program_id(0) == 0)
    def _():
        s_ref[...] = jnp.zeros_like(s_ref)

    c = D_CHUNK
    p = p_ref[...]
    q = p[:, 0:D_QK] * (D_DK ** -0.5)
    k = p[:, D_QK:2 * D_QK]
    v = p[:, 2 * D_QK:2 * D_QK + GROUP]
    og = p[:, 2 * D_QK + GROUP:2 * D_QK + 2 * GROUP]
    xg = p[:, 2 * D_QK + 2 * GROUP:]
    gl = -_softplus(-(_dot(xg.astype(BF16), w2_ref[...]) + gb_ref[...])) / D_GATE_TEMP
    row = _iota2((c, c), 0)
    col = _iota2((c, c), 1)
    cum = _dot3((col <= row).astype(BF16), gl)
    last = cum[c - 1:c, :]

    lane_head = _iota2((c, D_QK), 1) // D_DK
    row4 = _iota2((D_HEADS * c, c), 0) % c
    col4 = _iota2((D_HEADS * c, c), 1)
    levels = range(D_LEVELS)
    blk = [c >> lvl for lvl in levels]
    rho = [_dot3((col == (row // b) * b + (b // 2 - 1)).astype(BF16), cum) for b in blk]
    qs = [q * jnp.exp(jnp.minimum(cum - r, 0.0)) for r in rho] + [q]
    ks = [k * jnp.exp(jnp.minimum(r - cum, 0.0)) for r in rho] + [k]
    masks = [jnp.logical_and(row4 // b == col4 // b,
                             jnp.logical_and(row4 % b >= b // 2, col4 % b < b // 2)) for b in blk]
    masks.append(row4 == col4)
    heads = range(D_HEADS)
    prods = [_dot_nt(jnp.concatenate([jnp.where(lane_head == h, x, 0.0) for h in heads], axis=0).astype(BF16),
                     y.astype(BF16)) for x, y in zip(qs, ks)]
    scores = jnp.zeros((D_HEADS * c, c), F32)
    for mask, pr in zip(masks, prods):
        scores = scores + jnp.where(mask, pr, 0.0)
    sc = scores.astype(BF16)

    qe = (q * jnp.exp(cum)).astype(BF16)
    kd = k * jnp.exp(last - cum)
    e_last = jnp.exp(last)
    vh = [v[:, h * D_DV:(h + 1) * D_DV].astype(BF16) for h in heads]
    st = [s_ref[h] for h in heads]
    o = [_dot(sc[h * c:(h + 1) * c, :], vh[h]) + _dot_nt(qe, st[h].astype(BF16)) for h in heads]
    upd = [_dot_tn(vh[h], jnp.where(lane_head == h, kd, 0.0).astype(BF16)) for h in heads]
    for h in heads:
        vs = slice(h * D_DV, (h + 1) * D_DV)
        s_ref[h] = st[h] * e_last + upd[h]
        on = o[h] * lax.rsqrt(jnp.mean(o[h] * o[h], axis=-1, keepdims=True) + EPS) * ng_ref[...]
        ogh = og[:, vs]
        o_ref[:, vs] = (on * (ogh * _sigmoid(ogh))).astype(o_ref.dtype)


def _gla(pd, w2, gate_b, norm_g):
    s = pd.shape[0]
    c = D_CHUNK
    return pl.pallas_call(
        _gla_kernel,
        name="gla",
        grid=(s // c,),
        in_specs=[pl.BlockSpec((c, D_W), lambda i: (i, 0)),
                  pl.BlockSpec((LANES, D_QK), lambda i: (0, 0)),
                  pl.BlockSpec((1, D_QK), lambda i: (0, 0)),
                  pl.BlockSpec((1, D_DV), lambda i: (0, 0))],
        out_specs=pl.BlockSpec((c, GROUP), lambda i: (i, 0)),
        out_shape=jax.ShapeDtypeStruct((s, GROUP), BF16),
        scratch_shapes=[pltpu.VMEM((D_HEADS, D_DV, D_QK), F32)],
        compiler_params=_params("arbitrary"),
    )(pd, w2, gate_b.reshape(1, D_QK), norm_g.reshape(1, D_DV))


def _ffn_up_kernel(h_ref, wg_ref, wu_ref, cwg_ref, cwu_ref, cbg_ref, cbu_ref, o_ref, gbuf, ubuf):
    tm = h_ref.shape[0]

    @pl.when(pl.program_id(1) == 0)
    def _():
        gbuf[0:8, :] = jnp.zeros((8, gbuf.shape[1]), F32)
        ubuf[0:8, :] = jnp.zeros((8, ubuf.shape[1]), F32)

    def conv(buf, w_ref, cw_ref, cb_ref):
        buf[8:8 + tm, :] = _dot(h_ref[...], w_ref[...])
        acc = (cb_ref[...] + buf[6:6 + tm, :] * cw_ref[0:1, :] + buf[7:7 + tm, :] * cw_ref[1:2, :]
               + buf[8:8 + tm, :] * cw_ref[2:3, :])
        buf[0:8, :] = buf[tm:tm + 8, :]
        return acc

    gate = conv(gbuf, wg_ref, cwg_ref, cbg_ref)
    up = conv(ubuf, wu_ref, cwu_ref, cbu_ref)
    o_ref[...] = (gate * _sigmoid(gate) * up).astype(o_ref.dtype)


def _ffn_up(h, w_up, conv_w, conv_b, tm=512, tn=512):
    m, k = h.shape
    nj = D_FF // tn
    cb = conv_b.reshape(1, 2 * D_FF)
    return pl.pallas_call(
        _ffn_up_kernel,
        name="ffn_up",
        grid=(nj, m // tm),
        in_specs=[pl.BlockSpec((tm, k), lambda j, i: (i, 0)),
                  pl.BlockSpec((k, tn), lambda j, i: (0, j)),
                  pl.BlockSpec((k, tn), lambda j, i: (0, j + nj)),
                  pl.BlockSpec((3, tn), lambda j, i: (0, j)),
                  pl.BlockSpec((3, tn), lambda j, i: (0, j + nj)),
                  pl.BlockSpec((1, tn), lambda j, i: (0, j)),
                  pl.BlockSpec((1, tn), lambda j, i: (0, j + nj))],
        out_specs=pl.BlockSpec((tm, tn), lambda j, i: (i, j)),
        out_shape=jax.ShapeDtypeStruct((m, D_FF), BF16),
        scratch_shapes=[pltpu.VMEM((tm + 8, tn), F32), pltpu.VMEM((tm + 8, tn), F32)],
        compiler_params=_params("parallel", "arbitrary"),
    )(h, w_up, w_up, conv_w, conv_w, cb, cb)


def _pad_cols(w, width):
    return jnp.pad(w, ((0, 0), (0, width - w.shape[1])))


def _pad_rows(w, height):
    return jnp.pad(w, ((0, height - w.shape[0]), (0, 0)))


def _split_w_in(w):
    a_in = 2 * GROUP
    b0 = a_in
    wa = w[:, :a_in]
    rkv = w[:, b0:b0 + 3 * GROUP]
    xw = w[:, b0 + 3 * GROUP:b0 + 3 * GROUP + 32]
    xa = w[:, b0 + 3 * GROUP + 32:b0 + 3 * GROUP + 64]
    xg = w[:, b0 + 3 * GROUP + 64:b0 + 3 * GROUP + 160]
    wb = jnp.concatenate([rkv, _pad_cols(xw, LANES), _pad_cols(xa, LANES), _pad_cols(xg, LANES)], axis=1)
    c0 = b0 + 3 * GROUP + 160
    wc = w[:, c0:c0 + 3 * GROUP]
    d0 = c0 + 3 * GROUP
    qkv = w[:, d0:d0 + 2 * D_QK + GROUP]
    dxg = w[:, d0 + 2 * D_QK + GROUP:d0 + 2 * D_QK + GROUP + 16]
    og = w[:, d0 + 2 * D_QK + GROUP + 16:d0 + 2 * D_QK + 2 * GROUP + 16]
    wd = jnp.concatenate([qkv, og, _pad_cols(dxg, LANES)], axis=1)
    return [t.astype(BF16) for t in (wa, wb, wc, wd)]


def _split_mu(mu):
    rkv = mu[:3 * GROUP]
    pads = [jnp.pad(mu[3 * GROUP + lo:3 * GROUP + hi], (0, LANES - (hi - lo))) for lo, hi in ((0, 32), (32, 64), (64, 160))]
    return jnp.concatenate([rkv] + pads).reshape(1, B_W)


def kernel(x, mix_norm_g, w_in, a_ln_g, a_ln_b, a_ws, a_bs, b_mu, b_w0, b_w2, b_a0, b_a2, b_g2, b_kk, b_ka, b_rk, b_lnx_g, b_lnx_b, d_gate_w2, d_gate_b, d_norm_g, w_out, ffn_norm_g, w_up, conv_w, conv_b, w_down, final_norm_g):
    bsz, s, d = x.shape
    depth = w_in.shape[0]
    cos, sin = _rope_tables(s)
    outs = []
    for bi in range(bsz):
        xc = x[bi]
        for l in range(depth):
            h = _rmsnorm(xc, mix_norm_g[l], BF16)
            wa, wb, wc, wd = _split_w_in(w_in[l])
            pa = _matmul(h, wa, F32)
            pb = _matmul(h, wb, F32)
            pc = _matmul(h, wc, F32)
            pd = _matmul(h, wd, F32)
            ya = _gmlp(pa, a_ln_g[l], a_ln_b[l], a_ws[l], a_bs[l])
            yb = _rwkv(pb, _split_mu(b_mu[l]), b_w0[l], _pad_rows(b_w2[l], LANES).astype(BF16), b_a0[l],
                       _pad_rows(b_a2[l], LANES).astype(BF16), _pad_rows(b_g2[l], LANES).astype(BF16),
                       b_kk[l], b_ka[l], b_rk[l], b_lnx_g[l], b_lnx_b[l])
            yct = _moba(pc, cos, sin)
            yd = _gla(pd, _pad_rows(d_gate_w2[l], LANES).astype(BF16), d_gate_b[l], d_norm_g[l])
            xc = _outproj(ya, yb, yct, yd, w_out[l].astype(BF16), xc)
            h2 = _rmsnorm(xc, ffn_norm_g[l], BF16)
            act = _ffn_up(h2, w_up[l].astype(BF16), conv_w[l], conv_b[l])
            xc = _matmul_res(act, w_down[l].astype(BF16), xc)
        outs.append(_rmsnorm(xc, final_norm_g, F32))
    return jnp.stack(outs, axis=0)
```

```python
import functools
import math

import jax
import jax.numpy as jnp
import numpy as np
from jax import lax
from jax.experimental import pallas as pl
from jax.experimental.pallas import tpu as pltpu

F32 = jnp.float32
BF16 = jnp.bfloat16

EPS = 1e-6
GROUP = 512
LANES = 128
VMEM_LIMIT = 56 * 1024 * 1024

A_HEADS, A_CH, A_CHUNK = 4, 128, 128
B_HEADS, B_HD, B_CHUNK = 8, 64, 64
B_LNX_EPS = 64e-5
C_HEADS, C_HD, C_BLOCK, C_TOPK = 8, 64, 256, 3
ROPE_THETA = 10000.0
D_HEADS, D_DK, D_DV, D_CHUNK = 4, 64, 128, 64
D_GATE_TEMP = 16.0
D_FF = 5632
NEG = -1e30
LOG2E = 1.4426950408889634


def _params(*sem):
    return pltpu.CompilerParams(dimension_semantics=sem, vmem_limit_bytes=VMEM_LIMIT)


def _dot(a, b):
    return jnp.dot(a, b, preferred_element_type=F32)


def _dot_nt(a, b):
    return lax.dot_general(a, b, (((1,), (1,)), ((), ())), preferred_element_type=F32)


def _dot_tn(a, b):
    return lax.dot_general(a, b, (((0,), (0,)), ((), ())), preferred_element_type=F32)


def _split3(x):
    hi = x.astype(BF16)
    r1 = x - hi.astype(F32)
    mid = r1.astype(BF16)
    lo = (r1 - mid.astype(F32)).astype(BF16)
    return hi, mid, lo


def _dot3(sel, x):
    hi, mid, lo = _split3(x)
    return _dot(sel, hi) + _dot(sel, mid) + _dot(sel, lo)


def _dot3_nt(a, b):
    ah, am, al = _split3(a)
    bh, bm, bl = _split3(b)
    return (_dot_nt(ah, bh) + (_dot_nt(ah, bm) + _dot_nt(am, bh))
            + (_dot_nt(am, bm) + _dot_nt(ah, bl) + _dot_nt(al, bh)))


def _sigmoid(x):
    return 1.0 / (1.0 + jnp.exp(-x))


def _softplus(x):
    return jnp.maximum(x, 0.0) + jnp.log(1.0 + jnp.exp(-jnp.abs(x)))


def _iota2(shape, axis):
    return lax.broadcasted_iota(jnp.int32, shape, axis)


def _rmsnorm_kernel(x_ref, g_ref, o_ref):
    x = x_ref[...]
    ms = jnp.mean(x * x, axis=-1, keepdims=True)
    o_ref[...] = (x * lax.rsqrt(ms + EPS) * g_ref[...]).astype(o_ref.dtype)


def _rmsnorm(x, g, out_dtype, tm=512):
    m, d = x.shape
    return pl.pallas_call(
        _rmsnorm_kernel,
        name="rmsnorm",
        grid=(m // tm,),
        in_specs=[pl.BlockSpec((tm, d), lambda i: (i, 0)),
                  pl.BlockSpec((1, d), lambda i: (0, 0))],
        out_specs=pl.BlockSpec((tm, d), lambda i: (i, 0)),
        out_shape=jax.ShapeDtypeStruct((m, d), out_dtype),
        compiler_params=_params("parallel"),
    )(x, g.reshape(1, d))


def _mm_kernel(x_ref, w_ref, o_ref):
    o_ref[...] = _dot(x_ref[...], w_ref[...]).astype(o_ref.dtype)


def _matmul(x, w, out_dtype, tm=512):
    m, k = x.shape
    n = w.shape[1]
    return pl.pallas_call(
        _mm_kernel,
        name="inproj",
        grid=(m // tm,),
        in_specs=[pl.BlockSpec((tm, k), lambda i: (i, 0)),
                  pl.BlockSpec((k, n), lambda i: (0, 0))],
        out_specs=pl.BlockSpec((tm, n), lambda i: (i, 0)),
        out_shape=jax.ShapeDtypeStruct((m, n), out_dtype),
        compiler_params=_params("parallel"),
    )(x, w)


def _mm_res_kernel(x_ref, w_ref, r_ref, o_ref):
    o_ref[...] = r_ref[...] + _dot(x_ref[...], w_ref[...])


def _matmul_res(x, w, res, tm=512, tn=512):
    m, k = x.shape
    n = w.shape[1]
    return pl.pallas_call(
        _mm_res_kernel,
        name="ffn_down",
        grid=(n // tn, m // tm),
        in_specs=[pl.BlockSpec((tm, k), lambda j, i: (i, 0)),
                  pl.BlockSpec((k, tn), lambda j, i: (0, j)),
                  pl.BlockSpec((tm, tn), lambda j, i: (i, j))],
        out_specs=pl.BlockSpec((tm, tn), lambda j, i: (i, j)),
        out_shape=jax.ShapeDtypeStruct((m, n), F32),
        compiler_params=_params("parallel", "parallel"),
    )(x, w, res)


def _outproj_kernel(ya_ref, yb_ref, yct_ref, yd_ref, w_ref, x_ref, g_ref, o_ref, h_ref):
    yc = yct_ref[...].T.astype(BF16)
    acc = _dot(ya_ref[...], w_ref[0:GROUP, :])
    acc += _dot(yb_ref[...], w_ref[GROUP:2 * GROUP, :])
    acc += _dot(yc, w_ref[2 * GROUP:3 * GROUP, :])
    acc += _dot(yd_ref[...], w_ref[3 * GROUP:4 * GROUP, :])
    x = x_ref[...] + acc
    o_ref[...] = x
    ms = jnp.mean(x * x, axis=-1, keepdims=True)
    h_ref[...] = (x * lax.rsqrt(ms + EPS) * g_ref[...]).astype(h_ref.dtype)


def _outproj(ya, yb, yct, yd, w, x, g, tm=256):
    m, n = x.shape
    row = pl.BlockSpec((tm, GROUP), lambda i: (i, 0))
    full = pl.BlockSpec((tm, n), lambda i: (i, 0))
    return pl.pallas_call(
        _outproj_kernel,
        name="outproj",
        grid=(m // tm,),
        in_specs=[row, row,
                  pl.BlockSpec((GROUP, tm), lambda i: (0, i)),
                  row,
                  pl.BlockSpec((4 * GROUP, n), lambda i: (0, 0)),
                  full,
                  pl.BlockSpec((1, n), lambda i: (0, 0))],
        out_specs=[full, full],
        out_shape=[jax.ShapeDtypeStruct((m, n), F32), jax.ShapeDtypeStruct((m, n), BF16)],
        compiler_params=_params("parallel"),
    )(ya, yb, yct, yd, w, x, g.reshape(1, n))


def _gmlp_kernel(p_ref, lng_ref, lnb_ref, ws_ref, bsb_ref, o_ref, *, nchunk):
    p = p_ref[...]
    z = 0.5 * p * (1.0 + jnp.tanh(math.sqrt(2.0 / math.pi) * (p + 0.044715 * (p * p * p))))
    u = z[:, :GROUP]
    v = z[:, GROUP:]
    mu = jnp.mean(v, axis=-1, keepdims=True)
    vc = v - mu
    var = jnp.mean(vc * vc, axis=-1, keepdims=True)
    vn = (vc * lax.rsqrt(var + EPS) * lng_ref[...] + lnb_ref[...]).astype(BF16)
    tri = _iota2((A_CHUNK, A_CHUNK), 1) <= _iota2((A_CHUNK, A_CHUNK), 0)
    for h in range(A_HEADS):
        w = jnp.where(tri, ws_ref[h], 0.0).astype(BF16)
        cs = slice(h * A_CH, (h + 1) * A_CH)
        for c in range(nchunk):
            rs = slice(c * A_CHUNK, (c + 1) * A_CHUNK)
            mixed = _dot(w, vn[rs, cs]) + bsb_ref[:, cs]
            o_ref[rs, cs] = (u[rs, cs] * mixed).astype(o_ref.dtype)


def _gmlp(pa, ln_g, ln_b, ws, bs, nchunk=4):
    s = pa.shape[0]
    tm = nchunk * A_CHUNK
    bsb = jnp.repeat(bs.T, A_CH, axis=1)
    vec = pl.BlockSpec((1, GROUP), lambda i: (0, 0))
    return pl.pallas_call(
        functools.partial(_gmlp_kernel, nchunk=nchunk),
        name="gmlp",
        grid=(s // tm,),
        in_specs=[pl.BlockSpec((tm, 2 * GROUP), lambda i: (i, 0)), vec, vec,
                  pl.BlockSpec((A_HEADS, A_CHUNK, A_CHUNK), lambda i: (0, 0, 0)),
                  pl.BlockSpec((A_CHUNK, GROUP), lambda i: (0, 0))],
        out_specs=pl.BlockSpec((tm, GROUP), lambda i: (i, 0)),
        out_shape=jax.ShapeDtypeStruct((s, GROUP), BF16),
        compiler_params=_params("parallel"),
    )(pa, ln_g.reshape(1, GROUP), ln_b.reshape(1, GROUP), ws, bsb)


B_W = 3 * GROUP + 3 * LANES
B_TM = 512
B_STATE_CHUNKS = 4


def _rwkv_prep_kernel(p_ref, pprev_ref, mu_ref, w0_ref, w2_ref, a0_ref, a2_ref, g2_ref,
                      kkw_ref, ka_ref, rk_ref,
                      at_ref, bt_ref, kt_ref, rt_ref, v_ref, bd_ref, kd_ref, pc_ref, g_ref, bonus_ref):
    i = pl.program_id(0)
    p = p_ref[...]
    tm = p.shape[0]
    last = jnp.where(i == 0, 0.0, pprev_ref[7:8, :])
    prev = jnp.where(_iota2(p.shape, 0) == 0, last, pltpu.roll(p, 1, axis=0))
    xs = p + (prev - p) * mu_ref[...]
    r = xs[:, 0:GROUP]
    k = xs[:, GROUP:2 * GROUP]
    v = xs[:, 2 * GROUP:3 * GROUP]
    xw = xs[:, 3 * GROUP:3 * GROUP + LANES]
    xa = xs[:, 3 * GROUP + LANES:3 * GROUP + 2 * LANES]
    xg = xs[:, 3 * GROUP + 2 * LANES:3 * GROUP + 3 * LANES]
    w = -_softplus(-(w0_ref[...] + _dot(jnp.tanh(xw).astype(BF16), w2_ref[...]))) - 0.5
    logd = -jnp.exp(w)
    a = _sigmoid(a0_ref[...] + _dot(xa.astype(BF16), a2_ref[...]))
    g_ref[...] = _dot(_sigmoid(xg).astype(BF16), g2_ref[...])

    lane_head = _iota2((GROUP, GROUP), 0) // B_HD == _iota2((GROUP, GROUP), 1) // B_HD
    seg = lane_head.astype(BF16)
    kk = k * kkw_ref[...]
    nrm = jnp.sqrt(_dot3_rhs(kk * kk, seg))
    kk = kk / jnp.maximum(nrm, 1e-12)
    k2 = k * (1.0 + (a - 1.0) * ka_ref[...])
    b = kk * a
    bonus_ref[...] = _dot3_rhs(r * k2 * rk_ref[...], seg) * v

    row = _iota2((tm, tm), 0)
    col = _iota2((tm, tm), 1)
    same = row // B_CHUNK == col // B_CHUNK
    cl = _dot3(jnp.logical_and(same, col <= row).astype(BF16), logd)
    cl_last = _dot3(same.astype(BF16), logd)
    e_neg = jnp.exp(-cl)
    e_end = jnp.exp(cl_last - cl)
    at = -kk * jnp.exp(cl - logd)
    bt = b * e_neg
    kt = k2 * e_neg
    rt = r * jnp.exp(cl)
    bd = b * e_end
    kd = k2 * e_end
    for h in range(B_HEADS):
        cs = slice(h * B_HD, (h + 1) * B_HD)
        at_ref[h] = at[:, cs].astype(BF16)
        bt_ref[h] = bt[:, cs].astype(BF16)
        kt_ref[h] = kt[:, cs].astype(BF16)
        rt_ref[h] = rt[:, cs].astype(BF16)
        v_ref[h] = v[:, cs].astype(BF16)
        bd_ref[h] = bd[:, cs].astype(BF16)
        kd_ref[h] = kd[:, cs].astype(BF16)
    pc = jnp.exp(cl_last)
    for c in range(tm // B_CHUNK):
        pc_ref[c] = pc[c * B_CHUNK:c * B_CHUNK + 1, :]


def _dot3_rhs(x, sel):
    hi, mid, lo = _split3(x)
    return _dot(hi, sel) + _dot(mid, sel) + _dot(lo, sel)


def _rwkv_chunk_kernel(at_ref, bt_ref, kt_ref, rt_ref, v_ref, bd_ref, kd_ref, r2_ref, y0_ref, m2_ref, sa_ref):
    c = B_CHUNK
    row = _iota2((2 * c, 2 * c), 0)
    col = _iota2((2 * c, 2 * c), 1) % c
    keep = jnp.logical_or(col < row % c, jnp.logical_and(row >= c, col == row % c))
    zeros = jnp.zeros((c, B_HD), BF16)
    heads = range(B_HEADS)
    big = [jnp.where(keep, _dot_nt(jnp.concatenate([at_ref[h], rt_ref[h]], axis=0),
                                   jnp.concatenate([bt_ref[h], kt_ref[h]], axis=0)), 0.0)
           for h in heads]
    n = [big[h][:c, :c] for h in heads]
    akv = [_dot(big[h][:c, c:].astype(BF16), v_ref[h]) for h in heads]
    x = [jnp.concatenate([at_ref[h].astype(F32), akv[h]], axis=1) for h in heads]
    for lvl in range(6):
        nb = [n[h].astype(BF16) for h in heads]
        x = [x[h] + _dot(nb[h], x[h].astype(BF16)) for h in heads]
        if lvl < 5:
            n = [_dot(nb[h], nb[h]) for h in heads]
    xb = [x[h].astype(BF16) for h in heads]
    ry = [_dot(big[h][c:, :].astype(BF16),
               jnp.concatenate([xb[h], jnp.concatenate([zeros, v_ref[h]], axis=1)], axis=0))
          for h in heads]
    ms = [_dot_tn(xb[h], bd_ref[h]) for h in heads]
    vk = [_dot_tn(v_ref[h], kd_ref[h]) for h in heads]
    for h in heads:
        r2_ref[h] = (rt_ref[h].astype(F32) + ry[h][:, :B_HD]).astype(BF16)
        y0_ref[h] = ry[h][:, B_HD:]
        m2_ref[0, h] = ms[h][:B_HD, :].astype(BF16)
        sa_ref[0, h] = ms[h][B_HD:, :] + vk[h]


def _rwkv_state_kernel(r2_ref, y0_ref, m2_ref, sa_ref, pc_ref, g_ref, bonus_ref, lng_ref, lnb_ref,
                       o_ref, s_ref, y_ref, *, nchunk):
    @pl.when(pl.program_id(0) == 0)
    def _():
        s_ref[...] = jnp.zeros_like(s_ref)

    c = B_CHUNK
    for j in range(nchunk):
        rs = slice(j * c, (j + 1) * c)
        pc = pc_ref[j]
        heads = range(B_HEADS)
        st = [s_ref[h] for h in heads]
        sb = [st[h].astype(BF16) for h in heads]
        sm = [_dot(sb[h], m2_ref[j, h]) for h in heads]
        ys = [_dot_nt(r2_ref[h, rs, :], sb[h]) for h in heads]
        for h in heads:
            cs = slice(h * B_HD, (h + 1) * B_HD)
            s_ref[h] = st[h] * pc[:, cs] + sm[h] + sa_ref[j, h]
            y = ys[h] + y0_ref[h, rs, :]
            ym = jnp.mean(y, axis=-1, keepdims=True)
            yc = y - ym
            yv = jnp.mean(yc * yc, axis=-1, keepdims=True)
            y_ref[rs, cs] = yc * lax.rsqrt(yv + B_LNX_EPS)
    y = y_ref[...] * lng_ref[...] + lnb_ref[...] + bonus_ref[...]
    o_ref[...] = (y * g_ref[...]).astype(o_ref.dtype)


def _rwkv(pb, mu, w0, w2, a0, a2, g2, k_k, k_a, r_k, lnx_g, lnx_b):
    s = pb.shape[0]
    tm = min(B_TM, s)
    nch = s // B_CHUNK
    vecw = pl.BlockSpec((1, B_W), lambda i: (0, 0))
    vec = pl.BlockSpec((1, GROUP), lambda i: (0, 0))
    lora = pl.BlockSpec((LANES, GROUP), lambda i: (0, 0))
    hm = pl.BlockSpec((B_HEADS, tm, B_HD), lambda i: (0, i, 0))
    full = pl.BlockSpec((tm, GROUP), lambda i: (i, 0))
    hm_shape = jax.ShapeDtypeStruct((B_HEADS, s, B_HD), BF16)
    outs = pl.pallas_call(
        _rwkv_prep_kernel,
        name="rwkv_prep",
        grid=(s // tm,),
        in_specs=[pl.BlockSpec((tm, B_W), lambda i: (i, 0)),
                  pl.BlockSpec((8, B_W), lambda i: (jnp.maximum(i * (tm // 8) - 1, 0), 0)),
                  vecw, vec, lora, vec, lora, lora, vec, vec, vec],
        out_specs=[hm] * 7 + [pl.BlockSpec((tm // B_CHUNK, 1, GROUP), lambda i: (i, 0, 0)), full, full],
        out_shape=[hm_shape] * 7 + [jax.ShapeDtypeStruct((nch, 1, GROUP), F32),
                                    jax.ShapeDtypeStruct((s, GROUP), F32),
                                    jax.ShapeDtypeStruct((s, GROUP), F32)],
        compiler_params=_params("parallel"),
    )(pb, pb, mu, w0.reshape(1, GROUP), w2, a0.reshape(1, GROUP), a2, g2,
      k_k.reshape(1, GROUP), k_a.reshape(1, GROUP), r_k.reshape(1, GROUP))
    at, bt, kt, rt, v, bd, kd, pc, g, bonus = outs

    c = B_CHUNK
    hmc = pl.BlockSpec((B_HEADS, c, B_HD), lambda i: (0, i, 0))
    sq = pl.BlockSpec((1, B_HEADS, B_HD, B_HD), lambda i: (i, 0, 0, 0))
    r2, y0, m2, sa = pl.pallas_call(
        _rwkv_chunk_kernel,
        name="rwkv_chunk",
        grid=(nch,),
        in_specs=[hmc] * 7,
        out_specs=[hmc, hmc, sq, sq],
        out_shape=[hm_shape, jax.ShapeDtypeStruct((B_HEADS, s, B_HD), F32),
                   jax.ShapeDtypeStruct((nch, B_HEADS, B_HD, B_HD), BF16),
                   jax.ShapeDtypeStruct((nch, B_HEADS, B_HD, B_HD), F32)],
        compiler_params=_params("parallel"),
    )(at, bt, kt, rt, v, bd, kd)

    nck = min(B_STATE_CHUNKS, nch)
    rows = nck * c
    hmr = pl.BlockSpec((B_HEADS, rows, B_HD), lambda i: (0, i, 0))
    sqr = pl.BlockSpec((nck, B_HEADS, B_HD, B_HD), lambda i: (i, 0, 0, 0))
    fullr = pl.BlockSpec((rows, GROUP), lambda i: (i, 0))
    return pl.pallas_call(
        functools.partial(_rwkv_state_kernel, nchunk=nck),
        name="rwkv_state",
        grid=(nch // nck,),
        in_specs=[hmr, hmr, sqr, sqr, pl.BlockSpec((nck, 1, GROUP), lambda i: (i, 0, 0)), fullr, fullr, vec, vec],
        out_specs=fullr,
        out_shape=jax.ShapeDtypeStruct((s, GROUP), BF16),
        scratch_shapes=[pltpu.VMEM((B_HEADS, B_HD, B_HD), F32), pltpu.VMEM((rows, GROUP), F32)],
        compiler_params=_params("arbitrary"),
    )(r2, y0, m2, sa, pc, g, bonus, lnx_g.reshape(1, GROUP), lnx_b.reshape(1, GROUP))


def _moba_prep_kernel(p_ref, cos_ref, sin_ref, q_ref, k_ref, vt_ref, bias_ref, kmean_ref):
    i = pl.program_id(0)
    nb = kmean_ref.shape[0]

    @pl.when(i == 0)
    def _():
        kmean_ref[...] = jnp.zeros_like(kmean_ref)

    p = p_ref[...]
    cos = cos_ref[...]
    sin = sin_ref[...]
    first_half = (_iota2((C_BLOCK, GROUP), 1) % C_HD) < C_HD // 2

    def rope(x):
        rot = jnp.where(first_half, pltpu.roll(x, GROUP - C_HD // 2, axis=1), pltpu.roll(x, C_HD // 2, axis=1))
        return x * cos + rot * sin

    q = rope(p[:, 0:GROUP]) * (C_HD ** -0.5)
    k = rope(p[:, GROUP:2 * GROUP])
    v = p[:, 2 * GROUP:3 * GROUP]
    vt = v.T
    for h in range(C_HEADS):
        cs = slice(h * C_HD, (h + 1) * C_HD)
        q_ref[h] = (q[:, cs] * LOG2E).astype(BF16)
        k_ref[h, 0] = k[:, cs].astype(BF16)
        vt_ref[h, 0] = vt[cs, :].astype(BF16)

    km = kmean_ref[...]
    lane_h = _iota2((nb, GROUP), 1) // C_HD
    km_heads = jnp.concatenate([jnp.where(lane_h == h, km, 0.0) for h in range(C_HEADS)], axis=0)
    sc = _dot3_nt(km_heads, q).reshape(C_HEADS, nb, C_BLOCK)
    blk_i = _iota2((C_HEADS, nb, C_BLOCK), 1)
    blk = blk_i.astype(F32)
    cur = jnp.where(blk_i < i, sc, -jnp.inf)
    bias = jnp.full((C_HEADS, nb, C_BLOCK), NEG, F32)
    for _ in range(C_TOPK):
        m = jnp.max(cur, axis=1, keepdims=True)
        first = jnp.min(jnp.where(cur == m, blk, float(nb)), axis=1, keepdims=True)
        pick = jnp.logical_and(blk == first, m > -jnp.inf)
        bias = jnp.where(pick, 0.0, bias)
        cur = jnp.where(pick, -jnp.inf, cur)
    bias_ref[...] = bias
    kmean_ref[pl.ds(i, 1), :] = jnp.mean(k, axis=0, keepdims=True)


C_UNROLL = 4


def _moba_attn_kernel(q_ref, k_ref, vt_ref, bias_ref, o_ref, s_buf):
    i = pl.program_id(1)
    nb = k_ref.shape[1]
    q = q_ref[0]
    blk_a = 2 * i
    blk_b = 2 * i + 1
    slot_a = nb + C_UNROLL - 2
    slot_b = nb + C_UNROLL - 1
    shape = (C_BLOCK, 2 * C_BLOCK)
    key = _iota2(shape, 0)
    qry = _iota2(shape, 1)
    bias_a = jnp.where(qry < C_BLOCK, jnp.where(key <= qry, 0.0, NEG), bias_ref[0, pl.ds(blk_a, 1), :])
    bias_b = jnp.where(jnp.logical_and(qry >= C_BLOCK, key <= qry - C_BLOCK), 0.0, NEG)
    st_a = _dot_nt(k_ref[0, blk_a], q) + bias_a
    st_b = _dot_nt(k_ref[0, blk_b], q) + bias_b
    s_buf[slot_a] = st_a
    s_buf[slot_b] = st_b
    m = jnp.maximum(jnp.max(st_a, axis=0, keepdims=True), jnp.max(st_b, axis=0, keepdims=True))
    npast = blk_a
    ngroups = (npast + C_UNROLL - 1) // C_UNROLL

    def scores(t, m):
        js = [t * C_UNROLL + u for u in range(C_UNROLL)]
        jcs = [jnp.minimum(j, nb - 1) for j in js]
        sts = [_dot_nt(k_ref[0, jc], q) for jc in jcs]
        for j, jc, st in zip(js, jcs, sts):
            st = st + jnp.where(j < npast, bias_ref[0, pl.ds(jc, 1), :], NEG)
            s_buf[j] = st
            m = jnp.maximum(m, jnp.max(st, axis=0, keepdims=True))
        return m

    m = lax.fori_loop(0, ngroups, scores, m)

    p_a = jnp.exp2(s_buf[slot_a] - m)
    p_b = jnp.exp2(s_buf[slot_b] - m)
    l = jnp.sum(p_a, axis=0, keepdims=True) + jnp.sum(p_b, axis=0, keepdims=True)
    acc = _dot(vt_ref[0, blk_a], p_a.astype(BF16)) + _dot(vt_ref[0, blk_b], p_b.astype(BF16))

    def values(t, carry):
        l, acc = carry
        js = [t * C_UNROLL + u for u in range(C_UNROLL)]
        ps = [jnp.exp2(s_buf[j] - m) for j in js]
        pv = [_dot(vt_ref[0, jnp.minimum(j, nb - 1)], p.astype(BF16)) for j, p in zip(js, ps)]
        for p, o in zip(ps, pv):
            l = l + jnp.sum(p, axis=0, keepdims=True)
            acc = acc + o
        return l, acc

    l, acc = lax.fori_loop(0, ngroups, values, (l, acc))
    o_ref[0] = acc / l


def _moba(pc, cos, sin):
    s = pc.shape[0]
    nb = s // C_BLOCK
    tile = pl.BlockSpec((C_BLOCK, GROUP), lambda i: (i, 0))
    q, k, vt, bias = pl.pallas_call(
        _moba_prep_kernel,
        name="moba_prep",
        grid=(nb,),
        in_specs=[pl.BlockSpec((C_BLOCK, 3 * GROUP), lambda i: (i, 0)), tile, tile],
        out_specs=[pl.BlockSpec((C_HEADS, C_BLOCK, C_HD), lambda i: (0, i, 0)),
                   pl.BlockSpec((C_HEADS, 1, C_BLOCK, C_HD), lambda i: (0, i, 0, 0)),
                   pl.BlockSpec((C_HEADS, 1, C_HD, C_BLOCK), lambda i: (0, i, 0, 0)),
                   pl.BlockSpec((C_HEADS, nb, C_BLOCK), lambda i: (0, 0, i))],
        out_shape=[jax.ShapeDtypeStruct((C_HEADS, s, C_HD), BF16),
                   jax.ShapeDtypeStruct((C_HEADS, nb, C_BLOCK, C_HD), BF16),
                   jax.ShapeDtypeStruct((C_HEADS, nb, C_HD, C_BLOCK), BF16),
                   jax.ShapeDtypeStruct((C_HEADS, nb, s), F32)],
        scratch_shapes=[pltpu.VMEM((nb, GROUP), F32)],
        compiler_params=_params("arbitrary"),
    )(pc, cos, sin)
    ot = pl.pallas_call(
        _moba_attn_kernel,
        grid=(C_HEADS, nb // 2),
        in_specs=[pl.BlockSpec((1, 2 * C_BLOCK, C_HD), lambda h, i: (h, i, 0)),
                  pl.BlockSpec((1, nb, C_BLOCK, C_HD), lambda h, i: (h, 0, 0, 0)),
                  pl.BlockSpec((1, nb, C_HD, C_BLOCK), lambda h, i: (h, 0, 0, 0)),
                  pl.BlockSpec((1, nb, 2 * C_BLOCK), lambda h, i: (h, 0, i))],
        out_specs=pl.BlockSpec((1, C_HD, 2 * C_BLOCK), lambda h, i: (h, 0, i)),
        out_shape=jax.ShapeDtypeStruct((C_HEADS, C_HD, s), F32),
        scratch_shapes=[pltpu.VMEM((nb + C_UNROLL, C_BLOCK, 2 * C_BLOCK), F32)],
        compiler_params=_params("parallel", "parallel"),
        name="moba_attn",
    )(q, k, vt, bias)
    return ot.reshape(GROUP, s)


def _rope_tables(s):
    half = C_HD // 2
    inv = ROPE_THETA ** (-jnp.arange(half, dtype=F32) / half)
    ang = jnp.arange(s, dtype=F32)[:, None] * inv[None, :]
    cos = jnp.cos(ang)
    sin = jnp.sin(ang)
    cos_h = jnp.concatenate([cos, cos], axis=1)
    sin_h = jnp.concatenate([-sin, sin], axis=1)
    return jnp.tile(cos_h, (1, C_HEADS)), jnp.tile(sin_h, (1, C_HEADS))


D_W = 2 * D_HEADS * D_DK + 2 * GROUP + LANES
D_QK = D_HEADS * D_DK
D_LEVELS = 6


def _gla_kernel(p_ref, w2_ref, gb_ref, ng_ref, o_ref, s_ref):
    @pl.when(pl.program_id(0) == 0)
    def _():
        s_ref[...] = jnp.zeros_like(s_ref)

    c = D_CHUNK
    p = p_ref[...]
    q = p[:, 0:D_QK] * (D_DK ** -0.5)
    k = p[:, D_QK:2 * D_QK]
    v = p[:, 2 * D_QK:2 * D_QK + GROUP]
    og = p[:, 2 * D_QK + GROUP:2 * D_QK + 2 * GROUP]
    xg = p[:, 2 * D_QK + 2 * GROUP:]
    gl = -_softplus(-(_dot(xg.astype(BF16), w2_ref[...]) + gb_ref[...])) / D_GATE_TEMP
    row = _iota2((c, c), 0)
    col = _iota2((c, c), 1)
    cum = _dot3((col <= row).astype(BF16), gl)
    last = cum[c - 1:c, :]

    lane_head = _iota2((c, D_QK), 1) // D_DK
    row4 = _iota2((D_HEADS * c, c), 0) % c
    col4 = _iota2((D_HEADS * c, c), 1)
    levels = range(D_LEVELS)
    blk = [c >> lvl for lvl in levels]
    rho = [_dot3((col == (row // b) * b + (b // 2 - 1)).astype(BF16), cum) for b in blk]
    qs = [q * jnp.exp(jnp.minimum(cum - r, 0.0)) for r in rho] + [q]
    ks = [k * jnp.exp(jnp.minimum(r - cum, 0.0)) for r in rho] + [k]
    masks = [jnp.logical_and(row4 // b == col4 // b,
                             jnp.logical_and(row4 % b >= b // 2, col4 % b < b // 2)) for b in blk]
    masks.append(row4 == col4)
    heads = range(D_HEADS)
    prods = [_dot_nt(jnp.concatenate([jnp.where(lane_head == h, x, 0.0) for h in heads], axis=0).astype(BF16),
                     y.astype(BF16)) for x, y in zip(qs, ks)]
    scores = jnp.zeros((D_HEADS * c, c), F32)
    for mask, pr in zip(masks, prods):
        scores = scores + jnp.where(mask, pr, 0.0)
    sc = scores.astype(BF16)

    qe = (q * jnp.exp(cum)).astype(BF16)
    kd = k * jnp.exp(last - cum)
    e_last = jnp.exp(last)
    vh = [v[:, h * D_DV:(h + 1) * D_DV].astype(BF16) for h in heads]
    st = [s_ref[h] for h in heads]
    o = [_dot(sc[h * c:(h + 1) * c, :], vh[h]) + _dot_nt(qe, st[h].astype(BF16)) for h in heads]
    upd = [_dot_tn(vh[h], jnp.where(lane_head == h, kd, 0.0).astype(BF16)) for h in heads]
    for h in heads:
        vs = slice(h * D_DV, (h + 1) * D_DV)
        s_ref[h] = st[h] * e_last + upd[h]
        on = o[h] * lax.rsqrt(jnp.mean(o[h] * o[h], axis=-1, keepdims=True) + EPS) * ng_ref[...]
        ogh = og[:, vs]
        o_ref[:, vs] = (on * (ogh * _sigmoid(ogh))).astype(o_ref.dtype)


def _gla(pd, w2, gate_b, norm_g):
    s = pd.shape[0]
    c = D_CHUNK
    return pl.pallas_call(
        _gla_kernel,
        name="gla",
        grid=(s // c,),
        in_specs=[pl.BlockSpec((c, D_W), lambda i: (i, 0)),
                  pl.BlockSpec((LANES, D_QK), lambda i: (0, 0)),
                  pl.BlockSpec((1, D_QK), lambda i: (0, 0)),
                  pl.BlockSpec((1, D_DV), lambda i: (0, 0))],
        out_specs=pl.BlockSpec((c, GROUP), lambda i: (i, 0)),
        out_shape=jax.ShapeDtypeStruct((s, GROUP), BF16),
        scratch_shapes=[pltpu.VMEM((D_HEADS, D_DV, D_QK), F32)],
        compiler_params=_params("arbitrary"),
    )(pd, w2, gate_b.reshape(1, D_QK), norm_g.reshape(1, D_DV))


FFN_ROWS = 512


def _ffn_up_kernel(h_ref, wg_ref, wu_ref, cwg_ref, cwu_ref, cbg_ref, cbu_ref, o_ref, wgb, wub, gbuf, ubuf):
    tm = h_ref.shape[0]
    rows = min(tm, FFN_ROWS)

    @pl.when(pl.program_id(1) == 0)
    def _():
        wgb[...] = wg_ref[...].astype(BF16)
        wub[...] = wu_ref[...].astype(BF16)
        gbuf[0:8, :] = jnp.zeros((8, gbuf.shape[1]), F32)
        ubuf[0:8, :] = jnp.zeros((8, ubuf.shape[1]), F32)

    def conv(buf, w, cw_ref, cb_ref, r0):
        buf[8 + r0:8 + r0 + rows, :] = _dot(h_ref[r0:r0 + rows, :], w[...])
        return (cb_ref[...] + buf[6 + r0:6 + r0 + rows, :] * cw_ref[0:1, :]
                + buf[7 + r0:7 + r0 + rows, :] * cw_ref[1:2, :]
                + buf[8 + r0:8 + r0 + rows, :] * cw_ref[2:3, :])

    for r0 in range(0, tm, rows):
        gate = conv(gbuf, wgb, cwg_ref, cbg_ref, r0)
        up = conv(ubuf, wub, cwu_ref, cbu_ref, r0)
        o_ref[r0:r0 + rows, :] = (gate * _sigmoid(gate) * up).astype(o_ref.dtype)
    gbuf[0:8, :] = gbuf[tm:tm + 8, :]
    ubuf[0:8, :] = ubuf[tm:tm + 8, :]


def _ffn_up(h, w_up, conv_w, conv_b, tm=1024, tn=512):
    m, k = h.shape
    tm = min(tm, m)
    nj = D_FF // tn
    cb = conv_b.reshape(1, 2 * D_FF)
    return pl.pallas_call(
        _ffn_up_kernel,
        name="ffn_up",
        grid=(nj, m // tm),
        in_specs=[pl.BlockSpec((tm, k), lambda j, i: (i, 0)),
                  pl.BlockSpec((k, tn), lambda j, i: (0, j)),
                  pl.BlockSpec((k, tn), lambda j, i: (0, j + nj)),
                  pl.BlockSpec((3, tn), lambda j, i: (0, j)),
                  pl.BlockSpec((3, tn), lambda j, i: (0, j + nj)),
                  pl.BlockSpec((1, tn), lambda j, i: (0, j)),
                  pl.BlockSpec((1, tn), lambda j, i: (0, j + nj))],
        out_specs=pl.BlockSpec((tm, tn), lambda j, i: (i, j)),
        out_shape=jax.ShapeDtypeStruct((m, D_FF), BF16),
        scratch_shapes=[pltpu.VMEM((k, tn), BF16), pltpu.VMEM((k, tn), BF16),
                        pltpu.VMEM((tm + 8, tn), F32), pltpu.VMEM((tm + 8, tn), F32)],
        compiler_params=_params("parallel", "arbitrary"),
    )(h, w_up, w_up, conv_w, conv_w, cb, cb)


def _pad_cols(w, width):
    return jnp.pad(w, ((0, 0), (0, width - w.shape[1])))


def _pad_rows(w, height):
    return jnp.pad(w, ((0, height - w.shape[0]), (0, 0)))


def _split_w_in(w):
    a_in = 2 * GROUP
    b0 = a_in
    wa = w[:, :a_in]
    rkv = w[:, b0:b0 + 3 * GROUP]
    xw = w[:, b0 + 3 * GROUP:b0 + 3 * GROUP + 32]
    xa = w[:, b0 + 3 * GROUP + 32:b0 + 3 * GROUP + 64]
    xg = w[:, b0 + 3 * GROUP + 64:b0 + 3 * GROUP + 160]
    wb = jnp.concatenate([rkv, _pad_cols(xw, LANES), _pad_cols(xa, LANES), _pad_cols(xg, LANES)], axis=1)
    c0 = b0 + 3 * GROUP + 160
    wc = w[:, c0:c0 + 3 * GROUP]
    d0 = c0 + 3 * GROUP
    qkv = w[:, d0:d0 + 2 * D_QK + GROUP]
    dxg = w[:, d0 + 2 * D_QK + GROUP:d0 + 2 * D_QK + GROUP + 16]
    og = w[:, d0 + 2 * D_QK + GROUP + 16:d0 + 2 * D_QK + 2 * GROUP + 16]
    wd = jnp.concatenate([qkv, og, _pad_cols(dxg, LANES)], axis=1)
    return [t.astype(BF16) for t in (wa, wb, wc, wd)]


def _split_mu(mu):
    rkv = mu[:3 * GROUP]
    pads = [jnp.pad(mu[3 * GROUP + lo:3 * GROUP + hi], (0, LANES - (hi - lo))) for lo, hi in ((0, 32), (32, 64), (64, 160))]
    return jnp.concatenate([rkv] + pads).reshape(1, B_W)


def kernel(x, mix_norm_g, w_in, a_ln_g, a_ln_b, a_ws, a_bs, b_mu, b_w0, b_w2, b_a0, b_a2, b_g2, b_kk, b_ka, b_rk, b_lnx_g, b_lnx_b, d_gate_w2, d_gate_b, d_norm_g, w_out, ffn_norm_g, w_up, conv_w, conv_b, w_down, final_norm_g):
    bsz, s, d = x.shape
    depth = w_in.shape[0]
    cos, sin = _rope_tables(s)
    outs = []
    for bi in range(bsz):
        xc = x[bi]
        for l in range(depth):
            h = _rmsnorm(xc, mix_norm_g[l], BF16)
            wa, wb, wc, wd = _split_w_in(w_in[l])
            pa = _matmul(h, wa, F32)
            pb = _matmul(h, wb, F32)
            pc = _matmul(h, wc, F32)
            pd = _matmul(h, wd, F32)
            ya = _gmlp(pa, a_ln_g[l], a_ln_b[l], a_ws[l], a_bs[l])
            yb = _rwkv(pb, _split_mu(b_mu[l]), b_w0[l], _pad_rows(b_w2[l], LANES).astype(BF16), b_a0[l],
                       _pad_rows(b_a2[l], LANES).astype(BF16), _pad_rows(b_g2[l], LANES).astype(BF16),
                       b_kk[l], b_ka[l], b_rk[l], b_lnx_g[l], b_lnx_b[l])
            yct = _moba(pc, cos, sin)
            yd = _gla(pd, _pad_rows(d_gate_w2[l], LANES).astype(BF16), d_gate_b[l], d_norm_g[l])
            xc, h2 = _outproj(ya, yb, yct, yd, w_out[l].astype(BF16), xc, ffn_norm_g[l])
            act = _ffn_up(h2, w_up[l], conv_w[l], conv_b[l])
            xc = _matmul_res(act, w_down[l].astype(BF16), xc)
        outs.append(_rmsnorm(xc, final_norm_g, F32))
    return jnp.stack(outs, axis=0)
```

```python
import functools
import math

import jax
import jax.numpy as jnp
import numpy as np
from jax import lax
from jax.experimental import pallas as pl
from jax.experimental.pallas import tpu as pltpu

F32 = jnp.float32
BF16 = jnp.bfloat16

EPS = 1e-6
GROUP = 512
LANES = 128
VMEM_LIMIT = 56 * 1024 * 1024

A_HEADS, A_CH, A_CHUNK = 4, 128, 128
B_HEADS, B_HD, B_CHUNK = 8, 64, 64
B_LNX_EPS = 64e-5
C_HEADS, C_HD, C_BLOCK, C_TOPK = 8, 64, 256, 3
ROPE_THETA = 10000.0
D_HEADS, D_DK, D_DV, D_CHUNK = 4, 64, 128, 64
D_GATE_TEMP = 16.0
D_FF = 5632
NEG = -1e30
LOG2E = 1.4426950408889634


def _params(*sem):
    return pltpu.CompilerParams(dimension_semantics=sem, vmem_limit_bytes=VMEM_LIMIT)


def _dot(a, b):
    return jnp.dot(a, b, preferred_element_type=F32)


def _dot_nt(a, b):
    return lax.dot_general(a, b, (((1,), (1,)), ((), ())), preferred_element_type=F32)


def _dot_tn(a, b):
    return lax.dot_general(a, b, (((0,), (0,)), ((), ())), preferred_element_type=F32)


def _split3(x):
    hi = x.astype(BF16)
    r1 = x - hi.astype(F32)
    mid = r1.astype(BF16)
    lo = (r1 - mid.astype(F32)).astype(BF16)
    return hi, mid, lo


def _dot3(sel, x):
    hi, mid, lo = _split3(x)
    return _dot(sel, hi) + _dot(sel, mid) + _dot(sel, lo)


def _dot3_nt(a, b):
    ah, am, al = _split3(a)
    bh, bm, bl = _split3(b)
    return (_dot_nt(ah, bh) + (_dot_nt(ah, bm) + _dot_nt(am, bh))
            + (_dot_nt(am, bm) + _dot_nt(ah, bl) + _dot_nt(al, bh)))


def _sigmoid(x):
    return 1.0 / (1.0 + jnp.exp(-x))


def _softplus(x):
    return jnp.maximum(x, 0.0) + jnp.log(1.0 + jnp.exp(-jnp.abs(x)))


def _iota2(shape, axis):
    return lax.broadcasted_iota(jnp.int32, shape, axis)


def _rmsnorm_kernel(x_ref, g_ref, o_ref):
    x = x_ref[...]
    ms = jnp.mean(x * x, axis=-1, keepdims=True)
    o_ref[...] = (x * lax.rsqrt(ms + EPS) * g_ref[...]).astype(o_ref.dtype)


def _rmsnorm(x, g, out_dtype, tm=512):
    m, d = x.shape
    return pl.pallas_call(
        _rmsnorm_kernel,
        name="rmsnorm",
        grid=(m // tm,),
        in_specs=[pl.BlockSpec((tm, d), lambda i: (i, 0)),
                  pl.BlockSpec((1, d), lambda i: (0, 0))],
        out_specs=pl.BlockSpec((tm, d), lambda i: (i, 0)),
        out_shape=jax.ShapeDtypeStruct((m, d), out_dtype),
        compiler_params=_params("parallel"),
    )(x, g.reshape(1, d))


def _mm_kernel(x_ref, w_ref, o_ref):
    o_ref[...] = _dot(x_ref[...], w_ref[...]).astype(o_ref.dtype)


def _matmul(x, w, out_dtype, tm=512):
    m, k = x.shape
    n = w.shape[1]
    return pl.pallas_call(
        _mm_kernel,
        name="inproj",
        grid=(m // tm,),
        in_specs=[pl.BlockSpec((tm, k), lambda i: (i, 0)),
                  pl.BlockSpec((k, n), lambda i: (0, 0))],
        out_specs=pl.BlockSpec((tm, n), lambda i: (i, 0)),
        out_shape=jax.ShapeDtypeStruct((m, n), out_dtype),
        compiler_params=_params("parallel"),
    )(x, w)


def _mm_res_kernel(x_ref, w_ref, r_ref, o_ref, wb):
    @pl.when(pl.program_id(1) == 0)
    def _():
        wb[...] = w_ref[...].astype(BF16)

    o_ref[...] = r_ref[...] + _dot(x_ref[...], wb[...])


def _matmul_res(x, w_all, layer, res, tm=512, tn=512):
    m, k = x.shape
    n = w_all.shape[2]
    tm = min(tm, m)
    return pl.pallas_call(
        _mm_res_kernel,
        name="ffn_down",
        grid=(n // tn, m // tm),
        in_specs=[pl.BlockSpec((tm, k), lambda j, i: (i, 0)),
                  pl.BlockSpec((None, k, tn), lambda j, i: (layer, 0, j)),
                  pl.BlockSpec((tm, tn), lambda j, i: (i, j))],
        out_specs=pl.BlockSpec((tm, tn), lambda j, i: (i, j)),
        out_shape=jax.ShapeDtypeStruct((m, n), F32),
        scratch_shapes=[pltpu.VMEM((k, tn), BF16)],
        compiler_params=_params("parallel", "arbitrary"),
    )(x, w_all, res)


def _outproj_kernel(ya_ref, yb_ref, yct_ref, yd_ref, w_ref, x_ref, g_ref, o_ref, h_ref):
    yc = yct_ref[...].T.astype(BF16)
    acc = _dot(ya_ref[...], w_ref[0:GROUP, :])
    acc += _dot(yb_ref[...], w_ref[GROUP:2 * GROUP, :])
    acc += _dot(yc, w_ref[2 * GROUP:3 * GROUP, :])
    acc += _dot(yd_ref[...], w_ref[3 * GROUP:4 * GROUP, :])
    x = x_ref[...] + acc
    o_ref[...] = x
    ms = jnp.mean(x * x, axis=-1, keepdims=True)
    h_ref[...] = (x * lax.rsqrt(ms + EPS) * g_ref[...]).astype(h_ref.dtype)


def _outproj(ya, yb, yct, yd, w, x, g, tm=256):
    m, n = x.shape
    row = pl.BlockSpec((tm, GROUP), lambda i: (i, 0))
    full = pl.BlockSpec((tm, n), lambda i: (i, 0))
    return pl.pallas_call(
        _outproj_kernel,
        name="outproj",
        grid=(m // tm,),
        in_specs=[row, row,
                  pl.BlockSpec((GROUP, tm), lambda i: (0, i)),
                  row,
                  pl.BlockSpec((4 * GROUP, n), lambda i: (0, 0)),
                  full,
                  pl.BlockSpec((1, n), lambda i: (0, 0))],
        out_specs=[full, full],
        out_shape=[jax.ShapeDtypeStruct((m, n), F32), jax.ShapeDtypeStruct((m, n), BF16)],
        compiler_params=_params("parallel"),
    )(ya, yb, yct, yd, w, x, g.reshape(1, n))


def _gmlp_kernel(p_ref, lng_ref, lnb_ref, ws_ref, bsb_ref, o_ref, *, nchunk):
    p = p_ref[...]
    z = 0.5 * p * (1.0 + jnp.tanh(math.sqrt(2.0 / math.pi) * (p + 0.044715 * (p * p * p))))
    u = z[:, :GROUP]
    v = z[:, GROUP:]
    mu = jnp.mean(v, axis=-1, keepdims=True)
    vc = v - mu
    var = jnp.mean(vc * vc, axis=-1, keepdims=True)
    vn = (vc * lax.rsqrt(var + EPS) * lng_ref[...] + lnb_ref[...]).astype(BF16)
    tri = _iota2((A_CHUNK, A_CHUNK), 1) <= _iota2((A_CHUNK, A_CHUNK), 0)
    for h in range(A_HEADS):
        w = jnp.where(tri, ws_ref[h], 0.0).astype(BF16)
        cs = slice(h * A_CH, (h + 1) * A_CH)
        for c in range(nchunk):
            rs = slice(c * A_CHUNK, (c + 1) * A_CHUNK)
            mixed = _dot(w, vn[rs, cs]) + bsb_ref[:, cs]
            o_ref[rs, cs] = (u[rs, cs] * mixed).astype(o_ref.dtype)


def _gmlp(pa, ln_g, ln_b, ws, bs, nchunk=4):
    s = pa.shape[0]
    tm = nchunk * A_CHUNK
    bsb = jnp.repeat(bs.T, A_CH, axis=1)
    vec = pl.BlockSpec((1, GROUP), lambda i: (0, 0))
    return pl.pallas_call(
        functools.partial(_gmlp_kernel, nchunk=nchunk),
        name="gmlp",
        grid=(s // tm,),
        in_specs=[pl.BlockSpec((tm, 2 * GROUP), lambda i: (i, 0)), vec, vec,
                  pl.BlockSpec((A_HEADS, A_CHUNK, A_CHUNK), lambda i: (0, 0, 0)),
                  pl.BlockSpec((A_CHUNK, GROUP), lambda i: (0, 0))],
        out_specs=pl.BlockSpec((tm, GROUP), lambda i: (i, 0)),
        out_shape=jax.ShapeDtypeStruct((s, GROUP), BF16),
        compiler_params=_params("parallel"),
    )(pa, ln_g.reshape(1, GROUP), ln_b.reshape(1, GROUP), ws, bsb)


B_W = 3 * GROUP + 3 * LANES
B_TM = 512
B_STATE_CHUNKS = 4


def _rwkv_prep_kernel(p_ref, pprev_ref, mu_ref, w0_ref, w2_ref, a0_ref, a2_ref, g2_ref,
                      kkw_ref, ka_ref, rk_ref,
                      at_ref, bt_ref, kt_ref, rt_ref, v_ref, bd_ref, kd_ref, pc_ref, g_ref, bonus_ref):
    i = pl.program_id(0)
    p = p_ref[...]
    tm = p.shape[0]
    last = jnp.where(i == 0, 0.0, pprev_ref[7:8, :])
    prev = jnp.where(_iota2(p.shape, 0) == 0, last, pltpu.roll(p, 1, axis=0))
    xs = p + (prev - p) * mu_ref[...]
    r = xs[:, 0:GROUP]
    k = xs[:, GROUP:2 * GROUP]
    v = xs[:, 2 * GROUP:3 * GROUP]
    xw = xs[:, 3 * GROUP:3 * GROUP + LANES]
    xa = xs[:, 3 * GROUP + LANES:3 * GROUP + 2 * LANES]
    xg = xs[:, 3 * GROUP + 2 * LANES:3 * GROUP + 3 * LANES]
    w = -_softplus(-(w0_ref[...] + _dot(jnp.tanh(xw).astype(BF16), w2_ref[...]))) - 0.5
    logd = -jnp.exp(w)
    a = _sigmoid(a0_ref[...] + _dot(xa.astype(BF16), a2_ref[...]))
    g_ref[...] = _dot(_sigmoid(xg).astype(BF16), g2_ref[...])

    lane_head = _iota2((GROUP, GROUP), 0) // B_HD == _iota2((GROUP, GROUP), 1) // B_HD
    seg = lane_head.astype(BF16)
    kk = k * kkw_ref[...]
    nrm = jnp.sqrt(_dot3_rhs(kk * kk, seg))
    kk = kk / jnp.maximum(nrm, 1e-12)
    k2 = k * (1.0 + (a - 1.0) * ka_ref[...])
    b = kk * a
    bonus_ref[...] = _dot3_rhs(r * k2 * rk_ref[...], seg) * v

    row = _iota2((tm, tm), 0)
    col = _iota2((tm, tm), 1)
    same = row // B_CHUNK == col // B_CHUNK
    cl = _dot3(jnp.logical_and(same, col <= row).astype(BF16), logd)
    cl_last = _dot3(same.astype(BF16), logd)
    e_neg = jnp.exp(-cl)
    e_end = jnp.exp(cl_last - cl)
    at = -kk * jnp.exp(cl - logd)
    bt = b * e_neg
    kt = k2 * e_neg
    rt = r * jnp.exp(cl)
    bd = b * e_end
    kd = k2 * e_end
    for h in range(B_HEADS):
        cs = slice(h * B_HD, (h + 1) * B_HD)
        at_ref[h] = at[:, cs].astype(BF16)
        bt_ref[h] = bt[:, cs].astype(BF16)
        kt_ref[h] = kt[:, cs].astype(BF16)
        rt_ref[h] = rt[:, cs].astype(BF16)
        v_ref[h] = v[:, cs].astype(BF16)
        bd_ref[h] = bd[:, cs].astype(BF16)
        kd_ref[h] = kd[:, cs].astype(BF16)
    pc = jnp.exp(cl_last)
    for c in range(tm // B_CHUNK):
        pc_ref[c] = pc[c * B_CHUNK:c * B_CHUNK + 1, :]


def _dot3_rhs(x, sel):
    hi, mid, lo = _split3(x)
    return _dot(hi, sel) + _dot(mid, sel) + _dot(lo, sel)


def _rwkv_chunk_kernel(at_ref, bt_ref, kt_ref, rt_ref, v_ref, bd_ref, kd_ref, r2_ref, y0_ref, m2_ref, sa_ref):
    c = B_CHUNK
    row = _iota2((2 * c, 2 * c), 0)
    col = _iota2((2 * c, 2 * c), 1) % c
    keep = jnp.logical_or(col < row % c, jnp.logical_and(row >= c, col == row % c))
    zeros = jnp.zeros((c, B_HD), BF16)
    heads = range(B_HEADS)
    big = [jnp.where(keep, _dot_nt(jnp.concatenate([at_ref[h], rt_ref[h]], axis=0),
                                   jnp.concatenate([bt_ref[h], kt_ref[h]], axis=0)), 0.0)
           for h in heads]
    n = [big[h][:c, :c] for h in heads]
    akv = [_dot(big[h][:c, c:].astype(BF16), v_ref[h]) for h in heads]
    x = [jnp.concatenate([at_ref[h].astype(F32), akv[h]], axis=1) for h in heads]
    for lvl in range(6):
        nb = [n[h].astype(BF16) for h in heads]
        x = [x[h] + _dot(nb[h], x[h].astype(BF16)) for h in heads]
        if lvl < 5:
            n = [_dot(nb[h], nb[h]) for h in heads]
    xb = [x[h].astype(BF16) for h in heads]
    ry = [_dot(big[h][c:, :].astype(BF16),
               jnp.concatenate([xb[h], jnp.concatenate([zeros, v_ref[h]], axis=1)], axis=0))
          for h in heads]
    ms = [_dot_tn(xb[h], bd_ref[h]) for h in heads]
    vk = [_dot_tn(v_ref[h], kd_ref[h]) for h in heads]
    for h in heads:
        r2_ref[h] = (rt_ref[h].astype(F32) + ry[h][:, :B_HD]).astype(BF16)
        y0_ref[h] = ry[h][:, B_HD:]
        m2_ref[0, h] = ms[h][:B_HD, :].astype(BF16)
        sa_ref[0, h] = ms[h][B_HD:, :] + vk[h]


def _rwkv_state_kernel(r2_ref, y0_ref, m2_ref, sa_ref, pc_ref, g_ref, bonus_ref, lng_ref, lnb_ref,
                       o_ref, s_ref, y_ref, *, nchunk):
    @pl.when(pl.program_id(0) == 0)
    def _():
        s_ref[...] = jnp.zeros_like(s_ref)

    c = B_CHUNK
    for j in range(nchunk):
        rs = slice(j * c, (j + 1) * c)
        pc = pc_ref[j]
        heads = range(B_HEADS)
        st = [s_ref[h] for h in heads]
        sb = [st[h].astype(BF16) for h in heads]
        sm = [_dot(sb[h], m2_ref[j, h]) for h in heads]
        ys = [_dot_nt(r2_ref[h, rs, :], sb[h]) for h in heads]
        for h in heads:
            cs = slice(h * B_HD, (h + 1) * B_HD)
            s_ref[h] = st[h] * pc[:, cs] + sm[h] + sa_ref[j, h]
            y = ys[h] + y0_ref[h, rs, :]
            ym = jnp.mean(y, axis=-1, keepdims=True)
            yc = y - ym
            yv = jnp.mean(yc * yc, axis=-1, keepdims=True)
            y_ref[rs, cs] = yc * lax.rsqrt(yv + B_LNX_EPS)
    y = y_ref[...] * lng_ref[...] + lnb_ref[...] + bonus_ref[...]
    o_ref[...] = (y * g_ref[...]).astype(o_ref.dtype)


def _rwkv(pb, mu, w0, w2, a0, a2, g2, k_k, k_a, r_k, lnx_g, lnx_b):
    s = pb.shape[0]
    tm = min(B_TM, s)
    nch = s // B_CHUNK
    vecw = pl.BlockSpec((1, B_W), lambda i: (0, 0))
    vec = pl.BlockSpec((1, GROUP), lambda i: (0, 0))
    lora = pl.BlockSpec((LANES, GROUP), lambda i: (0, 0))
    hm = pl.BlockSpec((B_HEADS, tm, B_HD), lambda i: (0, i, 0))
    full = pl.BlockSpec((tm, GROUP), lambda i: (i, 0))
    hm_shape = jax.ShapeDtypeStruct((B_HEADS, s, B_HD), BF16)
    outs = pl.pallas_call(
        _rwkv_prep_kernel,
        name="rwkv_prep",
        grid=(s // tm,),
        in_specs=[pl.BlockSpec((tm, B_W), lambda i: (i, 0)),
                  pl.BlockSpec((8, B_W), lambda i: (jnp.maximum(i * (tm // 8) - 1, 0), 0)),
                  vecw, vec, lora, vec, lora, lora, vec, vec, vec],
        out_specs=[hm] * 7 + [pl.BlockSpec((tm // B_CHUNK, 1, GROUP), lambda i: (i, 0, 0)), full, full],
        out_shape=[hm_shape] * 7 + [jax.ShapeDtypeStruct((nch, 1, GROUP), F32),
                                    jax.ShapeDtypeStruct((s, GROUP), F32),
                                    jax.ShapeDtypeStruct((s, GROUP), F32)],
        compiler_params=_params("parallel"),
    )(pb, pb, mu, w0.reshape(1, GROUP), w2, a0.reshape(1, GROUP), a2, g2,
      k_k.reshape(1, GROUP), k_a.reshape(1, GROUP), r_k.reshape(1, GROUP))
    at, bt, kt, rt, v, bd, kd, pc, g, bonus = outs

    c = B_CHUNK
    hmc = pl.BlockSpec((B_HEADS, c, B_HD), lambda i: (0, i, 0))
    sq = pl.BlockSpec((1, B_HEADS, B_HD, B_HD), lambda i: (i, 0, 0, 0))
    r2, y0, m2, sa = pl.pallas_call(
        _rwkv_chunk_kernel,
        name="rwkv_chunk",
        grid=(nch,),
        in_specs=[hmc] * 7,
        out_specs=[hmc, hmc, sq, sq],
        out_shape=[hm_shape, jax.ShapeDtypeStruct((B_HEADS, s, B_HD), F32),
                   jax.ShapeDtypeStruct((nch, B_HEADS, B_HD, B_HD), BF16),
                   jax.ShapeDtypeStruct((nch, B_HEADS, B_HD, B_HD), F32)],
        compiler_params=_params("parallel"),
    )(at, bt, kt, rt, v, bd, kd)

    nck = min(B_STATE_CHUNKS, nch)
    rows = nck * c
    hmr = pl.BlockSpec((B_HEADS, rows, B_HD), lambda i: (0, i, 0))
    sqr = pl.BlockSpec((nck, B_HEADS, B_HD, B_HD), lambda i: (i, 0, 0, 0))
    fullr = pl.BlockSpec((rows, GROUP), lambda i: (i, 0))
    return pl.pallas_call(
        functools.partial(_rwkv_state_kernel, nchunk=nck),
        name="rwkv_state",
        grid=(nch // nck,),
        in_specs=[hmr, hmr, sqr, sqr, pl.BlockSpec((nck, 1, GROUP), lambda i: (i, 0, 0)), fullr, fullr, vec, vec],
        out_specs=fullr,
        out_shape=jax.ShapeDtypeStruct((s, GROUP), BF16),
        scratch_shapes=[pltpu.VMEM((B_HEADS, B_HD, B_HD), F32), pltpu.VMEM((rows, GROUP), F32)],
        compiler_params=_params("arbitrary"),
    )(r2, y0, m2, sa, pc, g, bonus, lnx_g.reshape(1, GROUP), lnx_b.reshape(1, GROUP))


def _moba_prep_kernel(p_ref, cos_ref, sin_ref, q_ref, k_ref, vt_ref, bias_ref, kmean_ref):
    i = pl.program_id(0)
    nb = kmean_ref.shape[0]

    @pl.when(i == 0)
    def _():
        kmean_ref[...] = jnp.zeros_like(kmean_ref)

    p = p_ref[...]
    cos = cos_ref[...]
    sin = sin_ref[...]
    first_half = (_iota2((C_BLOCK, GROUP), 1) % C_HD) < C_HD // 2

    def rope(x):
        rot = jnp.where(first_half, pltpu.roll(x, GROUP - C_HD // 2, axis=1), pltpu.roll(x, C_HD // 2, axis=1))
        return x * cos + rot * sin

    q = rope(p[:, 0:GROUP]) * (C_HD ** -0.5)
    k = rope(p[:, GROUP:2 * GROUP])
    v = p[:, 2 * GROUP:3 * GROUP]
    vt = v.T
    for h in range(C_HEADS):
        cs = slice(h * C_HD, (h + 1) * C_HD)
        q_ref[h] = (q[:, cs] * LOG2E).astype(BF16)
        k_ref[h, 0] = k[:, cs].astype(BF16)
        vt_ref[h, 0] = vt[cs, :].astype(BF16)

    km = kmean_ref[...]
    lane_h = _iota2((nb, GROUP), 1) // C_HD
    km_heads = jnp.concatenate([jnp.where(lane_h == h, km, 0.0) for h in range(C_HEADS)], axis=0)
    sc = _dot3_nt(km_heads, q).reshape(C_HEADS, nb, C_BLOCK)
    blk_i = _iota2((C_HEADS, nb, C_BLOCK), 1)
    blk = blk_i.astype(F32)
    cur = jnp.where(blk_i < i, sc, -jnp.inf)
    bias = jnp.full((C_HEADS, nb, C_BLOCK), NEG, F32)
    for _ in range(C_TOPK):
        m = jnp.max(cur, axis=1, keepdims=True)
        first = jnp.min(jnp.where(cur == m, blk, float(nb)), axis=1, keepdims=True)
        pick = jnp.logical_and(blk == first, m > -jnp.inf)
        bias = jnp.where(pick, 0.0, bias)
        cur = jnp.where(pick, -jnp.inf, cur)
    bias_ref[...] = bias
    kmean_ref[pl.ds(i, 1), :] = jnp.mean(k, axis=0, keepdims=True)


C_UNROLL = 8


def _moba_attn_kernel(q_ref, k_ref, vt_ref, bias_ref, o_ref, s_buf):
    i = pl.program_id(1)
    nb = k_ref.shape[1]
    q = q_ref[0]
    blk_a = 2 * i
    blk_b = 2 * i + 1
    slot_a = nb + C_UNROLL - 2
    slot_b = nb + C_UNROLL - 1
    shape = (C_BLOCK, 2 * C_BLOCK)
    key = _iota2(shape, 0)
    qry = _iota2(shape, 1)
    bias_a = jnp.where(qry < C_BLOCK, jnp.where(key <= qry, 0.0, NEG), bias_ref[0, pl.ds(blk_a, 1), :])
    bias_b = jnp.where(jnp.logical_and(qry >= C_BLOCK, key <= qry - C_BLOCK), 0.0, NEG)
    st_a = _dot_nt(k_ref[0, blk_a], q) + bias_a
    st_b = _dot_nt(k_ref[0, blk_b], q) + bias_b
    s_buf[slot_a] = st_a
    s_buf[slot_b] = st_b
    m = jnp.maximum(jnp.max(st_a, axis=0, keepdims=True), jnp.max(st_b, axis=0, keepdims=True))
    npast = blk_a
    ngroups = (npast + C_UNROLL - 1) // C_UNROLL

    def scores(t, m):
        js = [t * C_UNROLL + u for u in range(C_UNROLL)]
        jcs = [jnp.minimum(j, nb - 1) for j in js]
        sts = [_dot_nt(k_ref[0, jc], q) for jc in jcs]
        for j, jc, st in zip(js, jcs, sts):
            st = st + jnp.where(j < npast, bias_ref[0, pl.ds(jc, 1), :], NEG)
            s_buf[j] = st
            m = jnp.maximum(m, jnp.max(st, axis=0, keepdims=True))
        return m

    m = lax.fori_loop(0, ngroups, scores, m)

    p_a = jnp.exp2(s_buf[slot_a] - m)
    p_b = jnp.exp2(s_buf[slot_b] - m)
    l = jnp.sum(p_a, axis=0, keepdims=True) + jnp.sum(p_b, axis=0, keepdims=True)
    acc = _dot(vt_ref[0, blk_a], p_a.astype(BF16)) + _dot(vt_ref[0, blk_b], p_b.astype(BF16))

    def values(t, carry):
        l, acc = carry
        js = [t * C_UNROLL + u for u in range(C_UNROLL)]
        ps = [jnp.exp2(s_buf[j] - m) for j in js]
        pv = [_dot(vt_ref[0, jnp.minimum(j, nb - 1)], p.astype(BF16)) for j, p in zip(js, ps)]
        for p, o in zip(ps, pv):
            l = l + jnp.sum(p, axis=0, keepdims=True)
            acc = acc + o
        return l, acc

    l, acc = lax.fori_loop(0, ngroups, values, (l, acc))
    o_ref[0] = acc / l


def _moba(pc, cos, sin):
    s = pc.shape[0]
    nb = s // C_BLOCK
    tile = pl.BlockSpec((C_BLOCK, GROUP), lambda i: (i, 0))
    q, k, vt, bias = pl.pallas_call(
        _moba_prep_kernel,
        name="moba_prep",
        grid=(nb,),
        in_specs=[pl.BlockSpec((C_BLOCK, 3 * GROUP), lambda i: (i, 0)), tile, tile],
        out_specs=[pl.BlockSpec((C_HEADS, C_BLOCK, C_HD), lambda i: (0, i, 0)),
                   pl.BlockSpec((C_HEADS, 1, C_BLOCK, C_HD), lambda i: (0, i, 0, 0)),
                   pl.BlockSpec((C_HEADS, 1, C_HD, C_BLOCK), lambda i: (0, i, 0, 0)),
                   pl.BlockSpec((C_HEADS, nb, C_BLOCK), lambda i: (0, 0, i))],
        out_shape=[jax.ShapeDtypeStruct((C_HEADS, s, C_HD), BF16),
                   jax.ShapeDtypeStruct((C_HEADS, nb, C_BLOCK, C_HD), BF16),
                   jax.ShapeDtypeStruct((C_HEADS, nb, C_HD, C_BLOCK), BF16),
                   jax.ShapeDtypeStruct((C_HEADS, nb, s), F32)],
        scratch_shapes=[pltpu.VMEM((nb, GROUP), F32)],
        compiler_params=_params("arbitrary"),
    )(pc, cos, sin)
    ot = pl.pallas_call(
        _moba_attn_kernel,
        grid=(C_HEADS, nb // 2),
        in_specs=[pl.BlockSpec((1, 2 * C_BLOCK, C_HD), lambda h, i: (h, i, 0)),
                  pl.BlockSpec((1, nb, C_BLOCK, C_HD), lambda h, i: (h, 0, 0, 0)),
                  pl.BlockSpec((1, nb, C_HD, C_BLOCK), lambda h, i: (h, 0, 0, 0)),
                  pl.BlockSpec((1, nb, 2 * C_BLOCK), lambda h, i: (h, 0, i))],
        out_specs=pl.BlockSpec((1, C_HD, 2 * C_BLOCK), lambda h, i: (h, 0, i)),
        out_shape=jax.ShapeDtypeStruct((C_HEADS, C_HD, s), F32),
        scratch_shapes=[pltpu.VMEM((nb + C_UNROLL, C_BLOCK, 2 * C_BLOCK), F32)],
        compiler_params=_params("parallel", "parallel"),
        name="moba_attn",
    )(q, k, vt, bias)
    return ot.reshape(GROUP, s)


def _rope_tables(s):
    half = C_HD // 2
    inv = ROPE_THETA ** (-jnp.arange(half, dtype=F32) / half)
    ang = jnp.arange(s, dtype=F32)[:, None] * inv[None, :]
    cos = jnp.cos(ang)
    sin = jnp.sin(ang)
    cos_h = jnp.concatenate([cos, cos], axis=1)
    sin_h = jnp.concatenate([-sin, sin], axis=1)
    return jnp.tile(cos_h, (1, C_HEADS)), jnp.tile(sin_h, (1, C_HEADS))


D_W = 2 * D_HEADS * D_DK + 2 * GROUP + LANES
D_QK = D_HEADS * D_DK
D_LEVELS = 6


def _gla_kernel(p_ref, w2_ref, gb_ref, ng_ref, o_ref, s_ref):
    @pl.when(pl.program_id(0) == 0)
    def _():
        s_ref[...] = jnp.zeros_like(s_ref)

    c = D_CHUNK
    p = p_ref[...]
    q = p[:, 0:D_QK] * (D_DK ** -0.5)
    k = p[:, D_QK:2 * D_QK]
    v = p[:, 2 * D_QK:2 * D_QK + GROUP]
    og = p[:, 2 * D_QK + GROUP:2 * D_QK + 2 * GROUP]
    xg = p[:, 2 * D_QK + 2 * GROUP:]
    gl = -_softplus(-(_dot(xg.astype(BF16), w2_ref[...]) + gb_ref[...])) / D_GATE_TEMP
    row = _iota2((c, c), 0)
    col = _iota2((c, c), 1)
    cum = _dot3((col <= row).astype(BF16), gl)
    last = cum[c - 1:c, :]

    lane_head = _iota2((c, D_QK), 1) // D_DK
    row4 = _iota2((D_HEADS * c, c), 0) % c
    col4 = _iota2((D_HEADS * c, c), 1)
    levels = range(D_LEVELS)
    blk = [c >> lvl for lvl in levels]
    rho = [_dot3((col == (row // b) * b + (b // 2 - 1)).astype(BF16), cum) for b in blk]
    qs = [q * jnp.exp(jnp.minimum(cum - r, 0.0)) for r in rho] + [q]
    ks = [k * jnp.exp(jnp.minimum(r - cum, 0.0)) for r in rho] + [k]
    masks = [jnp.logical_and(row4 // b == col4 // b,
                             jnp.logical_and(row4 % b >= b // 2, col4 % b < b // 2)) for b in blk]
    masks.append(row4 == col4)
    heads = range(D_HEADS)
    prods = [_dot_nt(jnp.concatenate([jnp.where(lane_head == h, x, 0.0) for h in heads], axis=0).astype(BF16),
                     y.astype(BF16)) for x, y in zip(qs, ks)]
    scores = jnp.zeros((D_HEADS * c, c), F32)
    for mask, pr in zip(masks, prods):
        scores = scores + jnp.where(mask, pr, 0.0)
    sc = scores.astype(BF16)

    qe = (q * jnp.exp(cum)).astype(BF16)
    kd = k * jnp.exp(last - cum)
    e_last = jnp.exp(last)
    vh = [v[:, h * D_DV:(h + 1) * D_DV].astype(BF16) for h in heads]
    st = [s_ref[h] for h in heads]
    o = [_dot(sc[h * c:(h + 1) * c, :], vh[h]) + _dot_nt(qe, st[h].astype(BF16)) for h in heads]
    upd = [_dot_tn(vh[h], jnp.where(lane_head == h, kd, 0.0).astype(BF16)) for h in heads]
    for h in heads:
        vs = slice(h * D_DV, (h + 1) * D_DV)
        s_ref[h] = st[h] * e_last + upd[h]
        on = o[h] * lax.rsqrt(jnp.mean(o[h] * o[h], axis=-1, keepdims=True) + EPS) * ng_ref[...]
        ogh = og[:, vs]
        o_ref[:, vs] = (on * (ogh * _sigmoid(ogh))).astype(o_ref.dtype)


def _gla(pd, w2, gate_b, norm_g):
    s = pd.shape[0]
    c = D_CHUNK
    return pl.pallas_call(
        _gla_kernel,
        name="gla",
        grid=(s // c,),
        in_specs=[pl.BlockSpec((c, D_W), lambda i: (i, 0)),
                  pl.BlockSpec((LANES, D_QK), lambda i: (0, 0)),
                  pl.BlockSpec((1, D_QK), lambda i: (0, 0)),
                  pl.BlockSpec((1, D_DV), lambda i: (0, 0))],
        out_specs=pl.BlockSpec((c, GROUP), lambda i: (i, 0)),
        out_shape=jax.ShapeDtypeStruct((s, GROUP), BF16),
        scratch_shapes=[pltpu.VMEM((D_HEADS, D_DV, D_QK), F32)],
        compiler_params=_params("arbitrary"),
    )(pd, w2, gate_b.reshape(1, D_QK), norm_g.reshape(1, D_DV))


FFN_ROWS = 512


def _ffn_up_kernel(h_ref, wg_ref, wu_ref, cwg_ref, cwu_ref, cbg_ref, cbu_ref, o_ref, wgb, wub, gbuf, ubuf):
    tm = h_ref.shape[0]
    rows = min(tm, FFN_ROWS)

    @pl.when(pl.program_id(1) == 0)
    def _():
        wgb[...] = wg_ref[...].astype(BF16)
        wub[...] = wu_ref[...].astype(BF16)
        gbuf[0:8, :] = jnp.zeros((8, gbuf.shape[1]), F32)
        ubuf[0:8, :] = jnp.zeros((8, ubuf.shape[1]), F32)

    def conv(buf, w, cw_ref, cb_ref, r0):
        buf[8 + r0:8 + r0 + rows, :] = _dot(h_ref[r0:r0 + rows, :], w[...])
        return (cb_ref[...] + buf[6 + r0:6 + r0 + rows, :] * cw_ref[0:1, :]
                + buf[7 + r0:7 + r0 + rows, :] * cw_ref[1:2, :]
                + buf[8 + r0:8 + r0 + rows, :] * cw_ref[2:3, :])

    for r0 in range(0, tm, rows):
        gate = conv(gbuf, wgb, cwg_ref, cbg_ref, r0)
        up = conv(ubuf, wub, cwu_ref, cbu_ref, r0)
        o_ref[r0:r0 + rows, :] = (gate * _sigmoid(gate) * up).astype(o_ref.dtype)
    gbuf[0:8, :] = gbuf[tm:tm + 8, :]
    ubuf[0:8, :] = ubuf[tm:tm + 8, :]


def _ffn_up(h, w_up_all, layer, conv_w, conv_b, tm=1024, tn=512):
    m, k = h.shape
    tm = min(tm, m)
    nj = D_FF // tn
    cb = conv_b.reshape(1, 2 * D_FF)
    return pl.pallas_call(
        _ffn_up_kernel,
        name="ffn_up",
        grid=(nj, m // tm),
        in_specs=[pl.BlockSpec((tm, k), lambda j, i: (i, 0)),
                  pl.BlockSpec((None, k, tn), lambda j, i: (layer, 0, j)),
                  pl.BlockSpec((None, k, tn), lambda j, i: (layer, 0, j + nj)),
                  pl.BlockSpec((3, tn), lambda j, i: (0, j)),
                  pl.BlockSpec((3, tn), lambda j, i: (0, j + nj)),
                  pl.BlockSpec((1, tn), lambda j, i: (0, j)),
                  pl.BlockSpec((1, tn), lambda j, i: (0, j + nj))],
        out_specs=pl.BlockSpec((tm, tn), lambda j, i: (i, j)),
        out_shape=jax.ShapeDtypeStruct((m, D_FF), BF16),
        scratch_shapes=[pltpu.VMEM((k, tn), BF16), pltpu.VMEM((k, tn), BF16),
                        pltpu.VMEM((tm + 8, tn), F32), pltpu.VMEM((tm + 8, tn), F32)],
        compiler_params=_params("parallel", "arbitrary"),
    )(h, w_up_all, w_up_all, conv_w, conv_w, cb, cb)


def _pad_cols(w, width):
    return jnp.pad(w, ((0, 0), (0, width - w.shape[1])))


def _pad_rows(w, height):
    return jnp.pad(w, ((0, height - w.shape[0]), (0, 0)))


def _split_w_in(w):
    a_in = 2 * GROUP
    b0 = a_in
    wa = w[:, :a_in]
    rkv = w[:, b0:b0 + 3 * GROUP]
    xw = w[:, b0 + 3 * GROUP:b0 + 3 * GROUP + 32]
    xa = w[:, b0 + 3 * GROUP + 32:b0 + 3 * GROUP + 64]
    xg = w[:, b0 + 3 * GROUP + 64:b0 + 3 * GROUP + 160]
    wb = jnp.concatenate([rkv, _pad_cols(xw, LANES), _pad_cols(xa, LANES), _pad_cols(xg, LANES)], axis=1)
    c0 = b0 + 3 * GROUP + 160
    wc = w[:, c0:c0 + 3 * GROUP]
    d0 = c0 + 3 * GROUP
    qkv = w[:, d0:d0 + 2 * D_QK + GROUP]
    dxg = w[:, d0 + 2 * D_QK + GROUP:d0 + 2 * D_QK + GROUP + 16]
    og = w[:, d0 + 2 * D_QK + GROUP + 16:d0 + 2 * D_QK + 2 * GROUP + 16]
    wd = jnp.concatenate([qkv, og, _pad_cols(dxg, LANES)], axis=1)
    return [t.astype(BF16) for t in (wa, wb, wc, wd)]


def _split_mu(mu):
    rkv = mu[:3 * GROUP]
    pads = [jnp.pad(mu[3 * GROUP + lo:3 * GROUP + hi], (0, LANES - (hi - lo))) for lo, hi in ((0, 32), (32, 64), (64, 160))]
    return jnp.concatenate([rkv] + pads).reshape(1, B_W)


def kernel(x, mix_norm_g, w_in, a_ln_g, a_ln_b, a_ws, a_bs, b_mu, b_w0, b_w2, b_a0, b_a2, b_g2, b_kk, b_ka, b_rk, b_lnx_g, b_lnx_b, d_gate_w2, d_gate_b, d_norm_g, w_out, ffn_norm_g, w_up, conv_w, conv_b, w_down, final_norm_g):
    bsz, s, d = x.shape
    depth = w_in.shape[0]
    cos, sin = _rope_tables(s)
    outs = []
    for bi in range(bsz):
        xc = x[bi]
        for l in range(depth):
            h = _rmsnorm(xc, mix_norm_g[l], BF16)
            wa, wb, wc, wd = _split_w_in(w_in[l])
            pa = _matmul(h, wa, F32)
            pb = _matmul(h, wb, F32)
            pc = _matmul(h, wc, F32)
            pd = _matmul(h, wd, F32)
            ya = _gmlp(pa, a_ln_g[l], a_ln_b[l], a_ws[l], a_bs[l])
            yb = _rwkv(pb, _split_mu(b_mu[l]), b_w0[l], _pad_rows(b_w2[l], LANES).astype(BF16), b_a0[l],
                       _pad_rows(b_a2[l], LANES).astype(BF16), _pad_rows(b_g2[l], LANES).astype(BF16),
                       b_kk[l], b_ka[l], b_rk[l], b_lnx_g[l], b_lnx_b[l])
            yct = _moba(pc, cos, sin)
            yd = _gla(pd, _pad_rows(d_gate_w2[l], LANES).astype(BF16), d_gate_b[l], d_norm_g[l])
            xc, h2 = _outproj(ya, yb, yct, yd, w_out[l].astype(BF16), xc, ffn_norm_g[l])
            act = _ffn_up(h2, w_up, l, conv_w[l], conv_b[l])
            xc = _matmul_res(act, w_down, l, xc)
        outs.append(_rmsnorm(xc, final_norm_g, F32))
    return jnp.stack(outs, axis=0)
```

```python
import functools
import math

import jax
import jax.numpy as jnp
import numpy as np
from jax import lax
from jax.experimental import pallas as pl
from jax.experimental.pallas import tpu as pltpu

F32 = jnp.float32
BF16 = jnp.bfloat16

EPS = 1e-6
GROUP = 512
LANES = 128
VMEM_LIMIT = 56 * 1024 * 1024

A_HEADS, A_CH, A_CHUNK = 4, 128, 128
B_HEADS, B_HD, B_CHUNK = 8, 64, 64
B_LNX_EPS = 64e-5
C_HEADS, C_HD, C_BLOCK, C_TOPK = 8, 64, 256, 3
ROPE_THETA = 10000.0
D_HEADS, D_DK, D_DV, D_CHUNK = 4, 64, 128, 64
D_GATE_TEMP = 16.0
D_FF = 5632
NEG = -1e30
LOG2E = 1.4426950408889634


def _params(*sem):
    return pltpu.CompilerParams(dimension_semantics=sem, vmem_limit_bytes=VMEM_LIMIT)


def _dot(a, b):
    return jnp.dot(a, b, preferred_element_type=F32)


def _dot_nt(a, b):
    return lax.dot_general(a, b, (((1,), (1,)), ((), ())), preferred_element_type=F32)


def _dot_tn(a, b):
    return lax.dot_general(a, b, (((0,), (0,)), ((), ())), preferred_element_type=F32)


def _split3(x):
    hi = x.astype(BF16)
    r1 = x - hi.astype(F32)
    mid = r1.astype(BF16)
    lo = (r1 - mid.astype(F32)).astype(BF16)
    return hi, mid, lo


def _dot3(sel, x):
    hi, mid, lo = _split3(x)
    return _dot(sel, hi) + _dot(sel, mid) + _dot(sel, lo)


def _dot3_nt(a, b):
    ah, am, al = _split3(a)
    bh, bm, bl = _split3(b)
    return (_dot_nt(ah, bh) + (_dot_nt(ah, bm) + _dot_nt(am, bh))
            + (_dot_nt(am, bm) + _dot_nt(ah, bl) + _dot_nt(al, bh)))


def _sigmoid(x):
    return 1.0 / (1.0 + jnp.exp(-x))


def _softplus(x):
    return jnp.maximum(x, 0.0) + jnp.log(1.0 + jnp.exp(-jnp.abs(x)))


def _iota2(shape, axis):
    return lax.broadcasted_iota(jnp.int32, shape, axis)


def _rmsnorm_kernel(x_ref, g_ref, o_ref):
    x = x_ref[...]
    ms = jnp.mean(x * x, axis=-1, keepdims=True)
    o_ref[...] = (x * lax.rsqrt(ms + EPS) * g_ref[...]).astype(o_ref.dtype)


def _rmsnorm(x, g, out_dtype, tm=512):
    m, d = x.shape
    return pl.pallas_call(
        _rmsnorm_kernel,
        name="rmsnorm",
        grid=(m // tm,),
        in_specs=[pl.BlockSpec((tm, d), lambda i: (i, 0)),
                  pl.BlockSpec((1, d), lambda i: (0, 0))],
        out_specs=pl.BlockSpec((tm, d), lambda i: (i, 0)),
        out_shape=jax.ShapeDtypeStruct((m, d), out_dtype),
        compiler_params=_params("parallel"),
    )(x, g.reshape(1, d))


def _mm_kernel(x_ref, w_ref, o_ref):
    o_ref[...] = _dot(x_ref[...], w_ref[...]).astype(o_ref.dtype)


def _matmul(x, w, out_dtype, tm=512):
    m, k = x.shape
    n = w.shape[1]
    return pl.pallas_call(
        _mm_kernel,
        name="inproj",
        grid=(m // tm,),
        in_specs=[pl.BlockSpec((tm, k), lambda i: (i, 0)),
                  pl.BlockSpec((k, n), lambda i: (0, 0))],
        out_specs=pl.BlockSpec((tm, n), lambda i: (i, 0)),
        out_shape=jax.ShapeDtypeStruct((m, n), out_dtype),
        compiler_params=_params("parallel"),
    )(x, w)


def _mm_res_kernel(x_ref, w_ref, r_ref, o_ref, wb):
    @pl.when(pl.program_id(1) == 0)
    def _():
        wb[...] = w_ref[...].astype(BF16)

    o_ref[...] = r_ref[...] + _dot(x_ref[...], wb[...])


def _matmul_res(x, w_all, layer, res, tm=512, tn=512):
    m, k = x.shape
    n = w_all.shape[2]
    tm = min(tm, m)
    return pl.pallas_call(
        _mm_res_kernel,
        name="ffn_down",
        grid=(n // tn, m // tm),
        in_specs=[pl.BlockSpec((tm, k), lambda j, i: (i, 0)),
                  pl.BlockSpec((None, k, tn), lambda j, i: (layer, 0, j)),
                  pl.BlockSpec((tm, tn), lambda j, i: (i, j))],
        out_specs=pl.BlockSpec((tm, tn), lambda j, i: (i, j)),
        out_shape=jax.ShapeDtypeStruct((m, n), F32),
        scratch_shapes=[pltpu.VMEM((k, tn), BF16)],
        compiler_params=_params("parallel", "arbitrary"),
    )(x, w_all, res)


def _outproj_kernel(ya_ref, yb_ref, yct_ref, yd_ref, w_ref, x_ref, g_ref, o_ref, h_ref):
    yc = yct_ref[...].T.astype(BF16)
    acc = _dot(ya_ref[...], w_ref[0:GROUP, :])
    acc += _dot(yb_ref[...], w_ref[GROUP:2 * GROUP, :])
    acc += _dot(yc, w_ref[2 * GROUP:3 * GROUP, :])
    acc += _dot(yd_ref[...], w_ref[3 * GROUP:4 * GROUP, :])
    x = x_ref[...] + acc
    o_ref[...] = x
    ms = jnp.mean(x * x, axis=-1, keepdims=True)
    h_ref[...] = (x * lax.rsqrt(ms + EPS) * g_ref[...]).astype(h_ref.dtype)


def _outproj(ya, yb, yct, yd, w, x, g, tm=256):
    m, n = x.shape
    row = pl.BlockSpec((tm, GROUP), lambda i: (i, 0))
    full = pl.BlockSpec((tm, n), lambda i: (i, 0))
    return pl.pallas_call(
        _outproj_kernel,
        name="outproj",
        grid=(m // tm,),
        in_specs=[row, row,
                  pl.BlockSpec((GROUP, tm), lambda i: (0, i)),
                  row,
                  pl.BlockSpec((4 * GROUP, n), lambda i: (0, 0)),
                  full,
                  pl.BlockSpec((1, n), lambda i: (0, 0))],
        out_specs=[full, full],
        out_shape=[jax.ShapeDtypeStruct((m, n), F32), jax.ShapeDtypeStruct((m, n), BF16)],
        compiler_params=_params("parallel"),
    )(ya, yb, yct, yd, w, x, g.reshape(1, n))


def _gmlp_kernel(p_ref, lng_ref, lnb_ref, ws_ref, bsb_ref, o_ref, *, nchunk):
    p = p_ref[...]
    z = 0.5 * p * (1.0 + jnp.tanh(math.sqrt(2.0 / math.pi) * (p + 0.044715 * (p * p * p))))
    u = z[:, :GROUP]
    v = z[:, GROUP:]
    mu = jnp.mean(v, axis=-1, keepdims=True)
    vc = v - mu
    var = jnp.mean(vc * vc, axis=-1, keepdims=True)
    vn = (vc * lax.rsqrt(var + EPS) * lng_ref[...] + lnb_ref[...]).astype(BF16)
    tri = _iota2((A_CHUNK, A_CHUNK), 1) <= _iota2((A_CHUNK, A_CHUNK), 0)
    for h in range(A_HEADS):
        w = jnp.where(tri, ws_ref[h], 0.0).astype(BF16)
        cs = slice(h * A_CH, (h + 1) * A_CH)
        for c in range(nchunk):
            rs = slice(c * A_CHUNK, (c + 1) * A_CHUNK)
            mixed = _dot(w, vn[rs, cs]) + bsb_ref[:, cs]
            o_ref[rs, cs] = (u[rs, cs] * mixed).astype(o_ref.dtype)


def _gmlp(pa, ln_g, ln_b, ws, bs, nchunk=4):
    s = pa.shape[0]
    tm = nchunk * A_CHUNK
    bsb = jnp.repeat(bs.T, A_CH, axis=1)
    vec = pl.BlockSpec((1, GROUP), lambda i: (0, 0))
    return pl.pallas_call(
        functools.partial(_gmlp_kernel, nchunk=nchunk),
        name="gmlp",
        grid=(s // tm,),
        in_specs=[pl.BlockSpec((tm, 2 * GROUP), lambda i: (i, 0)), vec, vec,
                  pl.BlockSpec((A_HEADS, A_CHUNK, A_CHUNK), lambda i: (0, 0, 0)),
                  pl.BlockSpec((A_CHUNK, GROUP), lambda i: (0, 0))],
        out_specs=pl.BlockSpec((tm, GROUP), lambda i: (i, 0)),
        out_shape=jax.ShapeDtypeStruct((s, GROUP), BF16),
        compiler_params=_params("parallel"),
    )(pa, ln_g.reshape(1, GROUP), ln_b.reshape(1, GROUP), ws, bsb)


B_W = 3 * GROUP + 3 * LANES
B_TM = 512
B_STATE_CHUNKS = 4
B_CHUNKS_PER_STEP = 4


def _rwkv_prep_kernel(p_ref, pprev_ref, mu_ref, w0_ref, w2_ref, a0_ref, a2_ref, g2_ref,
                      kkw_ref, ka_ref, rk_ref,
                      at_ref, bt_ref, kt_ref, rt_ref, v_ref, bd_ref, kd_ref, pc_ref, g_ref, bonus_ref):
    i = pl.program_id(0)
    p = p_ref[...]
    tm = p.shape[0]
    last = jnp.where(i == 0, 0.0, pprev_ref[7:8, :])
    prev = jnp.where(_iota2(p.shape, 0) == 0, last, pltpu.roll(p, 1, axis=0))
    xs = p + (prev - p) * mu_ref[...]
    r = xs[:, 0:GROUP]
    k = xs[:, GROUP:2 * GROUP]
    v = xs[:, 2 * GROUP:3 * GROUP]
    xw = xs[:, 3 * GROUP:3 * GROUP + LANES]
    xa = xs[:, 3 * GROUP + LANES:3 * GROUP + 2 * LANES]
    xg = xs[:, 3 * GROUP + 2 * LANES:3 * GROUP + 3 * LANES]
    w = -_softplus(-(w0_ref[...] + _dot(jnp.tanh(xw).astype(BF16), w2_ref[...]))) - 0.5
    logd = -jnp.exp(w)
    a = _sigmoid(a0_ref[...] + _dot(xa.astype(BF16), a2_ref[...]))
    g_ref[...] = _dot(_sigmoid(xg).astype(BF16), g2_ref[...])

    lane_head = _iota2((GROUP, GROUP), 0) // B_HD == _iota2((GROUP, GROUP), 1) // B_HD
    seg = lane_head.astype(BF16)
    kk = k * kkw_ref[...]
    nrm = jnp.sqrt(_dot3_rhs(kk * kk, seg))
    kk = kk / jnp.maximum(nrm, 1e-12)
    k2 = k * (1.0 + (a - 1.0) * ka_ref[...])
    b = kk * a
    bonus_ref[...] = _dot3_rhs(r * k2 * rk_ref[...], seg) * v

    row = _iota2((tm, tm), 0)
    col = _iota2((tm, tm), 1)
    same = row // B_CHUNK == col // B_CHUNK
    cl = _dot3(jnp.logical_and(same, col <= row).astype(BF16), logd)
    cl_last = _dot3(same.astype(BF16), logd)
    e_neg = jnp.exp(-cl)
    e_end = jnp.exp(cl_last - cl)
    at = -kk * jnp.exp(cl - logd)
    bt = b * e_neg
    kt = k2 * e_neg
    rt = r * jnp.exp(cl)
    bd = b * e_end
    kd = k2 * e_end
    for h in range(B_HEADS):
        cs = slice(h * B_HD, (h + 1) * B_HD)
        at_ref[h] = at[:, cs].astype(BF16)
        bt_ref[h] = bt[:, cs].astype(BF16)
        kt_ref[h] = kt[:, cs].astype(BF16)
        rt_ref[h] = rt[:, cs].astype(BF16)
        v_ref[h] = v[:, cs].astype(BF16)
        bd_ref[h] = bd[:, cs].astype(BF16)
        kd_ref[h] = kd[:, cs].astype(BF16)
    pc = jnp.exp(cl_last)
    for c in range(tm // B_CHUNK):
        pc_ref[c] = pc[c * B_CHUNK:c * B_CHUNK + 1, :]


def _dot3_rhs(x, sel):
    hi, mid, lo = _split3(x)
    return _dot(hi, sel) + _dot(mid, sel) + _dot(lo, sel)


def _rwkv_chunk_kernel(at_ref, bt_ref, kt_ref, rt_ref, v_ref, bd_ref, kd_ref, r2_ref, y0_ref, m2_ref, sa_ref):
    c = B_CHUNK
    row = _iota2((2 * c, 2 * c), 0)
    col = _iota2((2 * c, 2 * c), 1) % c
    keep = jnp.logical_or(col < row % c, jnp.logical_and(row >= c, col == row % c))
    zeros = jnp.zeros((c, B_HD), BF16)
    nck = m2_ref.shape[0]
    units = [(j, h) for j in range(nck) for h in range(B_HEADS)]
    idx = range(len(units))

    def ld(ref, u):
        j, h = units[u]
        return ref[h, j * c:(j + 1) * c, :]

    big = [jnp.where(keep, _dot_nt(jnp.concatenate([ld(at_ref, u), ld(rt_ref, u)], axis=0),
                                   jnp.concatenate([ld(bt_ref, u), ld(kt_ref, u)], axis=0)), 0.0)
           for u in idx]
    n = [big[u][:c, :c] for u in idx]
    akv = [_dot(big[u][:c, c:].astype(BF16), ld(v_ref, u)) for u in idx]
    x = [jnp.concatenate([ld(at_ref, u).astype(F32), akv[u]], axis=1) for u in idx]
    for lvl in range(6):
        nb = [n[u].astype(BF16) for u in idx]
        x = [x[u] + _dot(nb[u], x[u].astype(BF16)) for u in idx]
        if lvl < 5:
            n = [_dot(nb[u], nb[u]) for u in idx]
    xb = [x[u].astype(BF16) for u in idx]
    ry = [_dot(big[u][c:, :].astype(BF16),
               jnp.concatenate([xb[u], jnp.concatenate([zeros, ld(v_ref, u)], axis=1)], axis=0))
          for u in idx]
    ms = [_dot_tn(xb[u], ld(bd_ref, u)) for u in idx]
    vk = [_dot_tn(ld(v_ref, u), ld(kd_ref, u)) for u in idx]
    for u in idx:
        j, h = units[u]
        rs = slice(j * c, (j + 1) * c)
        r2_ref[h, rs, :] = (ld(rt_ref, u).astype(F32) + ry[u][:, :B_HD]).astype(BF16)
        y0_ref[h, rs, :] = ry[u][:, B_HD:]
        m2_ref[j, h] = ms[u][:B_HD, :].astype(BF16)
        sa_ref[j, h] = ms[u][B_HD:, :] + vk[u]


def _rwkv_state_kernel(r2_ref, y0_ref, m2_ref, sa_ref, pc_ref, g_ref, bonus_ref, lng_ref, lnb_ref,
                       o_ref, s_ref, y_ref, *, nchunk):
    @pl.when(pl.program_id(0) == 0)
    def _():
        s_ref[...] = jnp.zeros_like(s_ref)

    c = B_CHUNK
    for j in range(nchunk):
        rs = slice(j * c, (j + 1) * c)
        pc = pc_ref[j]
        heads = range(B_HEADS)
        st = [s_ref[h] for h in heads]
        sb = [st[h].astype(BF16) for h in heads]
        sm = [_dot(sb[h], m2_ref[j, h]) for h in heads]
        ys = [_dot_nt(r2_ref[h, rs, :], sb[h]) for h in heads]
        for h in heads:
            cs = slice(h * B_HD, (h + 1) * B_HD)
            s_ref[h] = st[h] * pc[:, cs] + sm[h] + sa_ref[j, h]
            y = ys[h] + y0_ref[h, rs, :]
            ym = jnp.mean(y, axis=-1, keepdims=True)
            yc = y - ym
            yv = jnp.mean(yc * yc, axis=-1, keepdims=True)
            y_ref[rs, cs] = yc * lax.rsqrt(yv + B_LNX_EPS)
    y = y_ref[...] * lng_ref[...] + lnb_ref[...] + bonus_ref[...]
    o_ref[...] = (y * g_ref[...]).astype(o_ref.dtype)


def _rwkv(pb, mu, w0, w2, a0, a2, g2, k_k, k_a, r_k, lnx_g, lnx_b):
    s = pb.shape[0]
    tm = min(B_TM, s)
    nch = s // B_CHUNK
    vecw = pl.BlockSpec((1, B_W), lambda i: (0, 0))
    vec = pl.BlockSpec((1, GROUP), lambda i: (0, 0))
    lora = pl.BlockSpec((LANES, GROUP), lambda i: (0, 0))
    hm = pl.BlockSpec((B_HEADS, tm, B_HD), lambda i: (0, i, 0))
    full = pl.BlockSpec((tm, GROUP), lambda i: (i, 0))
    hm_shape = jax.ShapeDtypeStruct((B_HEADS, s, B_HD), BF16)
    outs = pl.pallas_call(
        _rwkv_prep_kernel,
        name="rwkv_prep",
        grid=(s // tm,),
        in_specs=[pl.BlockSpec((tm, B_W), lambda i: (i, 0)),
                  pl.BlockSpec((8, B_W), lambda i: (jnp.maximum(i * (tm // 8) - 1, 0), 0)),
                  vecw, vec, lora, vec, lora, lora, vec, vec, vec],
        out_specs=[hm] * 7 + [pl.BlockSpec((tm // B_CHUNK, 1, GROUP), lambda i: (i, 0, 0)), full, full],
        out_shape=[hm_shape] * 7 + [jax.ShapeDtypeStruct((nch, 1, GROUP), F32),
                                    jax.ShapeDtypeStruct((s, GROUP), F32),
                                    jax.ShapeDtypeStruct((s, GROUP), F32)],
        compiler_params=_params("parallel"),
    )(pb, pb, mu, w0.reshape(1, GROUP), w2, a0.reshape(1, GROUP), a2, g2,
      k_k.reshape(1, GROUP), k_a.reshape(1, GROUP), r_k.reshape(1, GROUP))
    at, bt, kt, rt, v, bd, kd, pc, g, bonus = outs

    c = B_CHUNK
    ncs = min(B_CHUNKS_PER_STEP, nch)
    hmc = pl.BlockSpec((B_HEADS, ncs * c, B_HD), lambda i: (0, i, 0))
    sq = pl.BlockSpec((ncs, B_HEADS, B_HD, B_HD), lambda i: (i, 0, 0, 0))
    r2, y0, m2, sa = pl.pallas_call(
        _rwkv_chunk_kernel,
        name="rwkv_chunk",
        grid=(nch // ncs,),
        in_specs=[hmc] * 7,
        out_specs=[hmc, hmc, sq, sq],
        out_shape=[hm_shape, jax.ShapeDtypeStruct((B_HEADS, s, B_HD), F32),
                   jax.ShapeDtypeStruct((nch, B_HEADS, B_HD, B_HD), BF16),
                   jax.ShapeDtypeStruct((nch, B_HEADS, B_HD, B_HD), F32)],
        compiler_params=_params("parallel"),
    )(at, bt, kt, rt, v, bd, kd)

    nck = min(B_STATE_CHUNKS, nch)
    rows = nck * c
    hmr = pl.BlockSpec((B_HEADS, rows, B_HD), lambda i: (0, i, 0))
    sqr = pl.BlockSpec((nck, B_HEADS, B_HD, B_HD), lambda i: (i, 0, 0, 0))
    fullr = pl.BlockSpec((rows, GROUP), lambda i: (i, 0))
    return pl.pallas_call(
        functools.partial(_rwkv_state_kernel, nchunk=nck),
        name="rwkv_state",
        grid=(nch // nck,),
        in_specs=[hmr, hmr, sqr, sqr, pl.BlockSpec((nck, 1, GROUP), lambda i: (i, 0, 0)), fullr, fullr, vec, vec],
        out_specs=fullr,
        out_shape=jax.ShapeDtypeStruct((s, GROUP), BF16),
        scratch_shapes=[pltpu.VMEM((B_HEADS, B_HD, B_HD), F32), pltpu.VMEM((rows, GROUP), F32)],
        compiler_params=_params("arbitrary"),
    )(r2, y0, m2, sa, pc, g, bonus, lnx_g.reshape(1, GROUP), lnx_b.reshape(1, GROUP))


def _moba_prep_kernel(p_ref, cos_ref, sin_ref, q_ref, k_ref, vt_ref, bias_ref, kmean_ref):
    i = pl.program_id(0)
    nb = kmean_ref.shape[0]

    @pl.when(i == 0)
    def _():
        kmean_ref[...] = jnp.zeros_like(kmean_ref)

    p = p_ref[...]
    cos = cos_ref[...]
    sin = sin_ref[...]
    first_half = (_iota2((C_BLOCK, GROUP), 1) % C_HD) < C_HD // 2

    def rope(x):
        rot = jnp.where(first_half, pltpu.roll(x, GROUP - C_HD // 2, axis=1), pltpu.roll(x, C_HD // 2, axis=1))
        return x * cos + rot * sin

    q = rope(p[:, 0:GROUP]) * (C_HD ** -0.5)
    k = rope(p[:, GROUP:2 * GROUP])
    v = p[:, 2 * GROUP:3 * GROUP]
    vt = v.T
    for h in range(C_HEADS):
        cs = slice(h * C_HD, (h + 1) * C_HD)
        q_ref[h] = (q[:, cs] * LOG2E).astype(BF16)
        k_ref[h, 0] = k[:, cs].astype(BF16)
        vt_ref[h, 0] = vt[cs, :].astype(BF16)

    km = kmean_ref[...]
    lane_h = _iota2((nb, GROUP), 1) // C_HD
    km_heads = jnp.concatenate([jnp.where(lane_h == h, km, 0.0) for h in range(C_HEADS)], axis=0)
    sc = _dot3_nt(km_heads, q).reshape(C_HEADS, nb, C_BLOCK)
    blk_i = _iota2((C_HEADS, nb, C_BLOCK), 1)
    blk = blk_i.astype(F32)
    cur = jnp.where(blk_i < i, sc, -jnp.inf)
    bias = jnp.full((C_HEADS, nb, C_BLOCK), NEG, F32)
    for _ in range(C_TOPK):
        m = jnp.max(cur, axis=1, keepdims=True)
        first = jnp.min(jnp.where(cur == m, blk, float(nb)), axis=1, keepdims=True)
        pick = jnp.logical_and(blk == first, m > -jnp.inf)
        bias = jnp.where(pick, 0.0, bias)
        cur = jnp.where(pick, -jnp.inf, cur)
    bias_ref[...] = bias
    kmean_ref[pl.ds(i, 1), :] = jnp.mean(k, axis=0, keepdims=True)


C_UNROLL = 8


def _moba_attn_kernel(q_ref, k_ref, vt_ref, bias_ref, o_ref, s_buf):
    i = pl.program_id(1)
    nb = k_ref.shape[1]
    q = q_ref[0]
    blk_a = 2 * i
    blk_b = 2 * i + 1
    slot_a = nb + C_UNROLL - 2
    slot_b = nb + C_UNROLL - 1
    shape = (C_BLOCK, 2 * C_BLOCK)
    key = _iota2(shape, 0)
    qry = _iota2(shape, 1)
    bias_a = jnp.where(qry < C_BLOCK, jnp.where(key <= qry, 0.0, NEG), bias_ref[0, pl.ds(blk_a, 1), :])
    bias_b = jnp.where(jnp.logical_and(qry >= C_BLOCK, key <= qry - C_BLOCK), 0.0, NEG)
    st_a = _dot_nt(k_ref[0, blk_a], q) + bias_a
    st_b = _dot_nt(k_ref[0, blk_b], q) + bias_b
    s_buf[slot_a] = st_a
    s_buf[slot_b] = st_b
    m = jnp.maximum(jnp.max(st_a, axis=0, keepdims=True), jnp.max(st_b, axis=0, keepdims=True))
    npast = blk_a
    ngroups = (npast + C_UNROLL - 1) // C_UNROLL

    def scores(t, m):
        js = [t * C_UNROLL + u for u in range(C_UNROLL)]
        jcs = [jnp.minimum(j, nb - 1) for j in js]
        sts = [_dot_nt(k_ref[0, jc], q) for jc in jcs]
        for j, jc, st in zip(js, jcs, sts):
            st = st + jnp.where(j < npast, bias_ref[0, pl.ds(jc, 1), :], NEG)
            s_buf[j] = st
            m = jnp.maximum(m, jnp.max(st, axis=0, keepdims=True))
        return m

    m = lax.fori_loop(0, ngroups, scores, m)

    p_a = jnp.exp2(s_buf[slot_a] - m)
    p_b = jnp.exp2(s_buf[slot_b] - m)
    l = jnp.sum(p_a, axis=0, keepdims=True) + jnp.sum(p_b, axis=0, keepdims=True)
    acc = _dot(vt_ref[0, blk_a], p_a.astype(BF16)) + _dot(vt_ref[0, blk_b], p_b.astype(BF16))

    def values(t, carry):
        l, acc = carry
        js = [t * C_UNROLL + u for u in range(C_UNROLL)]
        ps = [jnp.exp2(s_buf[j] - m) for j in js]
        pv = [_dot(vt_ref[0, jnp.minimum(j, nb - 1)], p.astype(BF16)) for j, p in zip(js, ps)]
        for p, o in zip(ps, pv):
            l = l + jnp.sum(p, axis=0, keepdims=True)
            acc = acc + o
        return l, acc

    l, acc = lax.fori_loop(0, ngroups, values, (l, acc))
    o_ref[0] = acc / l


def _moba(pc, cos, sin):
    s = pc.shape[0]
    nb = s // C_BLOCK
    tile = pl.BlockSpec((C_BLOCK, GROUP), lambda i: (i, 0))
    q, k, vt, bias = pl.pallas_call(
        _moba_prep_kernel,
        name="moba_prep",
        grid=(nb,),
        in_specs=[pl.BlockSpec((C_BLOCK, 3 * GROUP), lambda i: (i, 0)), tile, tile],
        out_specs=[pl.BlockSpec((C_HEADS, C_BLOCK, C_HD), lambda i: (0, i, 0)),
                   pl.BlockSpec((C_HEADS, 1, C_BLOCK, C_HD), lambda i: (0, i, 0, 0)),
                   pl.BlockSpec((C_HEADS, 1, C_HD, C_BLOCK), lambda i: (0, i, 0, 0)),
                   pl.BlockSpec((C_HEADS, nb, C_BLOCK), lambda i: (0, 0, i))],
        out_shape=[jax.ShapeDtypeStruct((C_HEADS, s, C_HD), BF16),
                   jax.ShapeDtypeStruct((C_HEADS, nb, C_BLOCK, C_HD), BF16),
                   jax.ShapeDtypeStruct((C_HEADS, nb, C_HD, C_BLOCK), BF16),
                   jax.ShapeDtypeStruct((C_HEADS, nb, s), F32)],
        scratch_shapes=[pltpu.VMEM((nb, GROUP), F32)],
        compiler_params=_params("arbitrary"),
    )(pc, cos, sin)
    ot = pl.pallas_call(
        _moba_attn_kernel,
        grid=(C_HEADS, nb // 2),
        in_specs=[pl.BlockSpec((1, 2 * C_BLOCK, C_HD), lambda h, i: (h, i, 0)),
                  pl.BlockSpec((1, nb, C_BLOCK, C_HD), lambda h, i: (h, 0, 0, 0)),
                  pl.BlockSpec((1, nb, C_HD, C_BLOCK), lambda h, i: (h, 0, 0, 0)),
                  pl.BlockSpec((1, nb, 2 * C_BLOCK), lambda h, i: (h, 0, i))],
        out_specs=pl.BlockSpec((1, C_HD, 2 * C_BLOCK), lambda h, i: (h, 0, i)),
        out_shape=jax.ShapeDtypeStruct((C_HEADS, C_HD, s), F32),
        scratch_shapes=[pltpu.VMEM((nb + C_UNROLL, C_BLOCK, 2 * C_BLOCK), F32)],
        compiler_params=_params("parallel", "parallel"),
        name="moba_attn",
    )(q, k, vt, bias)
    return ot.reshape(GROUP, s)


def _rope_tables(s):
    half = C_HD // 2
    inv = ROPE_THETA ** (-jnp.arange(half, dtype=F32) / half)
    ang = jnp.arange(s, dtype=F32)[:, None] * inv[None, :]
    cos = jnp.cos(ang)
    sin = jnp.sin(ang)
    cos_h = jnp.concatenate([cos, cos], axis=1)
    sin_h = jnp.concatenate([-sin, sin], axis=1)
    return jnp.tile(cos_h, (1, C_HEADS)), jnp.tile(sin_h, (1, C_HEADS))


D_W = 2 * D_HEADS * D_DK + 2 * GROUP + LANES
D_QK = D_HEADS * D_DK
D_LEVELS = 6
D_CHUNKS_PER_STEP = 4


def _gla_chunk_terms(p, w2, gb):
    c = D_CHUNK
    q = p[:, 0:D_QK] * (D_DK ** -0.5)
    k = p[:, D_QK:2 * D_QK]
    v = p[:, 2 * D_QK:2 * D_QK + GROUP]
    og = p[:, 2 * D_QK + GROUP:2 * D_QK + 2 * GROUP]
    xg = p[:, 2 * D_QK + 2 * GROUP:]
    gl = -_softplus(-(_dot(xg.astype(BF16), w2) + gb)) / D_GATE_TEMP
    row = _iota2((c, c), 0)
    col = _iota2((c, c), 1)
    cum = _dot3((col <= row).astype(BF16), gl)
    last = cum[c - 1:c, :]

    lane_head = _iota2((c, D_QK), 1) // D_DK
    row4 = _iota2((D_HEADS * c, c), 0) % c
    col4 = _iota2((D_HEADS * c, c), 1)
    levels = range(D_LEVELS)
    blk = [c >> lvl for lvl in levels]
    rho = [_dot3((col == (row // b) * b + (b // 2 - 1)).astype(BF16), cum) for b in blk]
    qs = [q * jnp.exp(jnp.minimum(cum - r, 0.0)) for r in rho] + [q]
    ks = [k * jnp.exp(jnp.minimum(r - cum, 0.0)) for r in rho] + [k]
    masks = [jnp.logical_and(row4 // b == col4 // b,
                             jnp.logical_and(row4 % b >= b // 2, col4 % b < b // 2)) for b in blk]
    masks.append(row4 == col4)
    heads = range(D_HEADS)
    prods = [_dot_nt(jnp.concatenate([jnp.where(lane_head == h, x, 0.0) for h in heads], axis=0).astype(BF16),
                     y.astype(BF16)) for x, y in zip(qs, ks)]
    scores = jnp.zeros((D_HEADS * c, c), F32)
    for mask, pr in zip(masks, prods):
        scores = scores + jnp.where(mask, pr, 0.0)
    sc = scores.astype(BF16)

    qe = (q * jnp.exp(cum)).astype(BF16)
    kd = k * jnp.exp(last - cum)
    e_last = jnp.exp(last)
    vh = [v[:, h * D_DV:(h + 1) * D_DV].astype(BF16) for h in heads]
    intra = [_dot(sc[h * c:(h + 1) * c, :], vh[h]) for h in heads]
    upd = [_dot_tn(vh[h], jnp.where(lane_head == h, kd, 0.0).astype(BF16)) for h in heads]
    return intra, upd, qe, e_last, og


def _gla_kernel(p_ref, w2_ref, gb_ref, ng_ref, o_ref, s_ref, *, nchunk):
    @pl.when(pl.program_id(0) == 0)
    def _():
        s_ref[...] = jnp.zeros_like(s_ref)

    c = D_CHUNK
    heads = range(D_HEADS)
    terms = [_gla_chunk_terms(p_ref[j * c:(j + 1) * c, :], w2_ref[...], gb_ref[...]) for j in range(nchunk)]
    st = [s_ref[h] for h in heads]
    for j, (intra, upd, qe, e_last, og) in enumerate(terms):
        rs = slice(j * c, (j + 1) * c)
        o = [intra[h] + _dot_nt(qe, st[h].astype(BF16)) for h in heads]
        st = [st[h] * e_last + upd[h] for h in heads]
        for h in heads:
            vs = slice(h * D_DV, (h + 1) * D_DV)
            on = o[h] * lax.rsqrt(jnp.mean(o[h] * o[h], axis=-1, keepdims=True) + EPS) * ng_ref[...]
            ogh = og[:, vs]
            o_ref[rs, vs] = (on * (ogh * _sigmoid(ogh))).astype(o_ref.dtype)
    for h in heads:
        s_ref[h] = st[h]


def _gla(pd, w2, gate_b, norm_g):
    s = pd.shape[0]
    nck = min(D_CHUNKS_PER_STEP, s // D_CHUNK)
    rows = nck * D_CHUNK
    return pl.pallas_call(
        functools.partial(_gla_kernel, nchunk=nck),
        name="gla",
        grid=(s // rows,),
        in_specs=[pl.BlockSpec((rows, D_W), lambda i: (i, 0)),
                  pl.BlockSpec((LANES, D_QK), lambda i: (0, 0)),
                  pl.BlockSpec((1, D_QK), lambda i: (0, 0)),
                  pl.BlockSpec((1, D_DV), lambda i: (0, 0))],
        out_specs=pl.BlockSpec((rows, GROUP), lambda i: (i, 0)),
        out_shape=jax.ShapeDtypeStruct((s, GROUP), BF16),
        scratch_shapes=[pltpu.VMEM((D_HEADS, D_DV, D_QK), F32)],
        compiler_params=_params("arbitrary"),
    )(pd, w2, gate_b.reshape(1, D_QK), norm_g.reshape(1, D_DV))


FFN_ROWS = 512


def _ffn_up_kernel(h_ref, wg_ref, wu_ref, cwg_ref, cwu_ref, cbg_ref, cbu_ref, o_ref, wgb, wub, gbuf, ubuf):
    tm = h_ref.shape[0]
    rows = min(tm, FFN_ROWS)

    @pl.when(pl.program_id(1) == 0)
    def _():
        wgb[...] = wg_ref[...].astype(BF16)
        wub[...] = wu_ref[...].astype(BF16)
        gbuf[0:8, :] = jnp.zeros((8, gbuf.shape[1]), F32)
        ubuf[0:8, :] = jnp.zeros((8, ubuf.shape[1]), F32)

    def conv(buf, w, cw_ref, cb_ref, r0):
        buf[8 + r0:8 + r0 + rows, :] = _dot(h_ref[r0:r0 + rows, :], w[...])
        return (cb_ref[...] + buf[6 + r0:6 + r0 + rows, :] * cw_ref[0:1, :]
                + buf[7 + r0:7 + r0 + rows, :] * cw_ref[1:2, :]
                + buf[8 + r0:8 + r0 + rows, :] * cw_ref[2:3, :])

    for r0 in range(0, tm, rows):
        gate = conv(gbuf, wgb, cwg_ref, cbg_ref, r0)
        up = conv(ubuf, wub, cwu_ref, cbu_ref, r0)
        o_ref[r0:r0 + rows, :] = (gate * _sigmoid(gate) * up).astype(o_ref.dtype)
    gbuf[0:8, :] = gbuf[tm:tm + 8, :]
    ubuf[0:8, :] = ubuf[tm:tm + 8, :]


def _ffn_up(h, w_up_all, layer, conv_w, conv_b, tm=1024, tn=512):
    m, k = h.shape
    tm = min(tm, m)
    nj = D_FF // tn
    cb = conv_b.reshape(1, 2 * D_FF)
    return pl.pallas_call(
        _ffn_up_kernel,
        name="ffn_up",
        grid=(nj, m // tm),
        in_specs=[pl.BlockSpec((tm, k), lambda j, i: (i, 0)),
                  pl.BlockSpec((None, k, tn), lambda j, i: (layer, 0, j)),
                  pl.BlockSpec((None, k, tn), lambda j, i: (layer, 0, j + nj)),
                  pl.BlockSpec((3, tn), lambda j, i: (0, j)),
                  pl.BlockSpec((3, tn), lambda j, i: (0, j + nj)),
                  pl.BlockSpec((1, tn), lambda j, i: (0, j)),
                  pl.BlockSpec((1, tn), lambda j, i: (0, j + nj))],
        out_specs=pl.BlockSpec((tm, tn), lambda j, i: (i, j)),
        out_shape=jax.ShapeDtypeStruct((m, D_FF), BF16),
        scratch_shapes=[pltpu.VMEM((k, tn), BF16), pltpu.VMEM((k, tn), BF16),
                        pltpu.VMEM((tm + 8, tn), F32), pltpu.VMEM((tm + 8, tn), F32)],
        compiler_params=_params("parallel", "arbitrary"),
    )(h, w_up_all, w_up_all, conv_w, conv_w, cb, cb)


def _pad_cols(w, width):
    return jnp.pad(w, ((0, 0), (0, width - w.shape[1])))


def _pad_rows(w, height):
    return jnp.pad(w, ((0, height - w.shape[0]), (0, 0)))


def _split_w_in(w):
    a_in = 2 * GROUP
    b0 = a_in
    wa = w[:, :a_in]
    rkv = w[:, b0:b0 + 3 * GROUP]
    xw = w[:, b0 + 3 * GROUP:b0 + 3 * GROUP + 32]
    xa = w[:, b0 + 3 * GROUP + 32:b0 + 3 * GROUP + 64]
    xg = w[:, b0 + 3 * GROUP + 64:b0 + 3 * GROUP + 160]
    wb = jnp.concatenate([rkv, _pad_cols(xw, LANES), _pad_cols(xa, LANES), _pad_cols(xg, LANES)], axis=1)
    c0 = b0 + 3 * GROUP + 160
    wc = w[:, c0:c0 + 3 * GROUP]
    d0 = c0 + 3 * GROUP
    qkv = w[:, d0:d0 + 2 * D_QK + GROUP]
    dxg = w[:, d0 + 2 * D_QK + GROUP:d0 + 2 * D_QK + GROUP + 16]
    og = w[:, d0 + 2 * D_QK + GROUP + 16:d0 + 2 * D_QK + 2 * GROUP + 16]
    wd = jnp.concatenate([qkv, og, _pad_cols(dxg, LANES)], axis=1)
    return [t.astype(BF16) for t in (wa, wb, wc, wd)]


def _split_mu(mu):
    rkv = mu[:3 * GROUP]
    pads = [jnp.pad(mu[3 * GROUP + lo:3 * GROUP + hi], (0, LANES - (hi - lo))) for lo, hi in ((0, 32), (32, 64), (64, 160))]
    return jnp.concatenate([rkv] + pads).reshape(1, B_W)


def kernel(x, mix_norm_g, w_in, a_ln_g, a_ln_b, a_ws, a_bs, b_mu, b_w0, b_w2, b_a0, b_a2, b_g2, b_kk, b_ka, b_rk, b_lnx_g, b_lnx_b, d_gate_w2, d_gate_b, d_norm_g, w_out, ffn_norm_g, w_up, conv_w, conv_b, w_down, final_norm_g):
    bsz, s, d = x.shape
    depth = w_in.shape[0]
    cos, sin = _rope_tables(s)
    outs = []
    for bi in range(bsz):
        xc = x[bi]
        for l in range(depth):
            h = _rmsnorm(xc, mix_norm_g[l], BF16)
            wa, wb, wc, wd = _split_w_in(w_in[l])
            pa = _matmul(h, wa, F32)
            pb = _matmul(h, wb, F32)
            pc = _matmul(h, wc, F32)
            pd = _matmul(h, wd, F32)
            ya = _gmlp(pa, a_ln_g[l], a_ln_b[l], a_ws[l], a_bs[l])
            yb = _rwkv(pb, _split_mu(b_mu[l]), b_w0[l], _pad_rows(b_w2[l], LANES).astype(BF16), b_a0[l],
                       _pad_rows(b_a2[l], LANES).astype(BF16), _pad_rows(b_g2[l], LANES).astype(BF16),
                       b_kk[l], b_ka[l], b_rk[l], b_lnx_g[l], b_lnx_b[l])
            yct = _moba(pc, cos, sin)
            yd = _gla(pd, _pad_rows(d_gate_w2[l], LANES).astype(BF16), d_gate_b[l], d_norm_g[l])
            xc, h2 = _outproj(ya, yb, yct, yd, w_out[l].astype(BF16), xc, ffn_norm_g[l])
            act = _ffn_up(h2, w_up, l, conv_w[l], conv_b[l])
            xc = _matmul_res(act, w_down, l, xc)
        outs.append(_rmsnorm(xc, final_norm_g, F32))
    return jnp.stack(outs, axis=0)
```

```python
import functools
import math

import jax
import jax.numpy as jnp
import numpy as np
from jax import lax
from jax.experimental import pallas as pl
from jax.experimental.pallas import tpu as pltpu

F32 = jnp.float32
BF16 = jnp.bfloat16

EPS = 1e-6
GROUP = 512
LANES = 128
VMEM_LIMIT = 56 * 1024 * 1024

A_HEADS, A_CH, A_CHUNK = 4, 128, 128
B_HEADS, B_HD, B_CHUNK = 8, 64, 64
B_LNX_EPS = 64e-5
C_HEADS, C_HD, C_BLOCK, C_TOPK = 8, 64, 256, 3
ROPE_THETA = 10000.0
D_HEADS, D_DK, D_DV, D_CHUNK = 4, 64, 128, 64
D_GATE_TEMP = 16.0
D_FF = 5632
NEG = -1e30
LOG2E = 1.4426950408889634


def _params(*sem):
    return pltpu.CompilerParams(dimension_semantics=sem, vmem_limit_bytes=VMEM_LIMIT)


def _dot(a, b):
    return jnp.dot(a, b, preferred_element_type=F32)


def _dot_nt(a, b):
    return lax.dot_general(a, b, (((1,), (1,)), ((), ())), preferred_element_type=F32)


def _dot_tn(a, b):
    return lax.dot_general(a, b, (((0,), (0,)), ((), ())), preferred_element_type=F32)


def _split3(x):
    hi = x.astype(BF16)
    r1 = x - hi.astype(F32)
    mid = r1.astype(BF16)
    lo = (r1 - mid.astype(F32)).astype(BF16)
    return hi, mid, lo


def _split2(x):
    hi = x.astype(BF16)
    return hi, (x - hi.astype(F32)).astype(BF16)


def _dot_split(sel, x):
    hi, lo = _split2(x)
    return _dot(sel, hi) + _dot(sel, lo)


def _dot3_nt(a, b):
    ah, am, al = _split3(a)
    bh, bm, bl = _split3(b)
    return (_dot_nt(ah, bh) + (_dot_nt(ah, bm) + _dot_nt(am, bh))
            + (_dot_nt(am, bm) + _dot_nt(ah, bl) + _dot_nt(al, bh)))


def _sigmoid(x):
    return 1.0 / (1.0 + jnp.exp(-x))


def _softplus(x):
    return jnp.maximum(x, 0.0) + jnp.log(1.0 + jnp.exp(-jnp.abs(x)))


def _iota2(shape, axis):
    return lax.broadcasted_iota(jnp.int32, shape, axis)


def _rmsnorm_kernel(x_ref, g_ref, o_ref):
    x = x_ref[...]
    ms = jnp.mean(x * x, axis=-1, keepdims=True)
    o_ref[...] = (x * lax.rsqrt(ms + EPS) * g_ref[...]).astype(o_ref.dtype)


def _rmsnorm(x, g, out_dtype, tm=512):
    m, d = x.shape
    return pl.pallas_call(
        _rmsnorm_kernel,
        name="rmsnorm",
        grid=(m // tm,),
        in_specs=[pl.BlockSpec((tm, d), lambda i: (i, 0)),
                  pl.BlockSpec((1, d), lambda i: (0, 0))],
        out_specs=pl.BlockSpec((tm, d), lambda i: (i, 0)),
        out_shape=jax.ShapeDtypeStruct((m, d), out_dtype),
        compiler_params=_params("parallel"),
    )(x, g.reshape(1, d))


def _mm_kernel(x_ref, w_ref, o_ref):
    o_ref[...] = _dot(x_ref[...], w_ref[...]).astype(o_ref.dtype)


def _matmul(x, w, out_dtype, tm=512):
    m, k = x.shape
    n = w.shape[1]
    return pl.pallas_call(
        _mm_kernel,
        name="inproj",
        grid=(m // tm,),
        in_specs=[pl.BlockSpec((tm, k), lambda i: (i, 0)),
                  pl.BlockSpec((k, n), lambda i: (0, 0))],
        out_specs=pl.BlockSpec((tm, n), lambda i: (i, 0)),
        out_shape=jax.ShapeDtypeStruct((m, n), out_dtype),
        compiler_params=_params("parallel"),
    )(x, w)


def _mm_res_kernel(x_ref, w_ref, r_ref, o_ref, wb):
    @pl.when(pl.program_id(1) == 0)
    def _():
        wb[...] = w_ref[...].astype(BF16)

    o_ref[...] = r_ref[...] + _dot(x_ref[...], wb[...])


def _matmul_res(x, w_all, layer, res, tm=512, tn=512):
    m, k = x.shape
    n = w_all.shape[2]
    tm = min(tm, m)
    return pl.pallas_call(
        _mm_res_kernel,
        name="ffn_down",
        grid=(n // tn, m // tm),
        in_specs=[pl.BlockSpec((tm, k), lambda j, i: (i, 0)),
                  pl.BlockSpec((None, k, tn), lambda j, i: (layer, 0, j)),
                  pl.BlockSpec((tm, tn), lambda j, i: (i, j))],
        out_specs=pl.BlockSpec((tm, tn), lambda j, i: (i, j)),
        out_shape=jax.ShapeDtypeStruct((m, n), F32),
        scratch_shapes=[pltpu.VMEM((k, tn), BF16)],
        compiler_params=_params("parallel", "arbitrary"),
    )(x, w_all, res)


def _outproj_kernel(ya_ref, yb_ref, yct_ref, yd_ref, w_ref, x_ref, g_ref, o_ref, h_ref):
    yc = yct_ref[...].T.astype(BF16)
    acc = _dot(ya_ref[...], w_ref[0:GROUP, :])
    acc += _dot(yb_ref[...], w_ref[GROUP:2 * GROUP, :])
    acc += _dot(yc, w_ref[2 * GROUP:3 * GROUP, :])
    acc += _dot(yd_ref[...], w_ref[3 * GROUP:4 * GROUP, :])
    x = x_ref[...] + acc
    o_ref[...] = x
    ms = jnp.mean(x * x, axis=-1, keepdims=True)
    h_ref[...] = (x * lax.rsqrt(ms + EPS) * g_ref[...]).astype(h_ref.dtype)


def _outproj(ya, yb, yct, yd, w, x, g, tm=256):
    m, n = x.shape
    row = pl.BlockSpec((tm, GROUP), lambda i: (i, 0))
    full = pl.BlockSpec((tm, n), lambda i: (i, 0))
    return pl.pallas_call(
        _outproj_kernel,
        name="outproj",
        grid=(m // tm,),
        in_specs=[row, row,
                  pl.BlockSpec((GROUP, tm), lambda i: (0, i)),
                  row,
                  pl.BlockSpec((4 * GROUP, n), lambda i: (0, 0)),
                  full,
                  pl.BlockSpec((1, n), lambda i: (0, 0))],
        out_specs=[full, full],
        out_shape=[jax.ShapeDtypeStruct((m, n), F32), jax.ShapeDtypeStruct((m, n), BF16)],
        compiler_params=_params("parallel"),
    )(ya, yb, yct, yd, w, x, g.reshape(1, n))


def _gmlp_kernel(p_ref, lng_ref, lnb_ref, ws_ref, bsb_ref, o_ref, *, nchunk):
    p = p_ref[...]
    z = 0.5 * p * (1.0 + jnp.tanh(math.sqrt(2.0 / math.pi) * (p + 0.044715 * (p * p * p))))
    u = z[:, :GROUP]
    v = z[:, GROUP:]
    mu = jnp.mean(v, axis=-1, keepdims=True)
    vc = v - mu
    var = jnp.mean(vc * vc, axis=-1, keepdims=True)
    vn = (vc * lax.rsqrt(var + EPS) * lng_ref[...] + lnb_ref[...]).astype(BF16)
    tri = _iota2((A_CHUNK, A_CHUNK), 1) <= _iota2((A_CHUNK, A_CHUNK), 0)
    for h in range(A_HEADS):
        w = jnp.where(tri, ws_ref[h], 0.0).astype(BF16)
        cs = slice(h * A_CH, (h + 1) * A_CH)
        for c in range(nchunk):
            rs = slice(c * A_CHUNK, (c + 1) * A_CHUNK)
            mixed = _dot(w, vn[rs, cs]) + bsb_ref[:, cs]
            o_ref[rs, cs] = (u[rs, cs] * mixed).astype(o_ref.dtype)


def _gmlp(pa, ln_g, ln_b, ws, bs, nchunk=4):
    s = pa.shape[0]
    tm = nchunk * A_CHUNK
    bsb = jnp.repeat(bs.T, A_CH, axis=1)
    vec = pl.BlockSpec((1, GROUP), lambda i: (0, 0))
    return pl.pallas_call(
        functools.partial(_gmlp_kernel, nchunk=nchunk),
        name="gmlp",
        grid=(s // tm,),
        in_specs=[pl.BlockSpec((tm, 2 * GROUP), lambda i: (i, 0)), vec, vec,
                  pl.BlockSpec((A_HEADS, A_CHUNK, A_CHUNK), lambda i: (0, 0, 0)),
                  pl.BlockSpec((A_CHUNK, GROUP), lambda i: (0, 0))],
        out_specs=pl.BlockSpec((tm, GROUP), lambda i: (i, 0)),
        out_shape=jax.ShapeDtypeStruct((s, GROUP), BF16),
        compiler_params=_params("parallel"),
    )(pa, ln_g.reshape(1, GROUP), ln_b.reshape(1, GROUP), ws, bsb)


B_W = 3 * GROUP + 3 * LANES
B_TM = 512
B_STATE_CHUNKS = 4
B_CHUNKS_PER_STEP = 4


def _rwkv_prep_kernel(p_ref, pprev_ref, mu_ref, w0_ref, w2_ref, a0_ref, a2_ref, g2_ref,
                      kkw_ref, ka_ref, rk_ref,
                      at_ref, bt_ref, kt_ref, rt_ref, v_ref, bd_ref, kd_ref, pc_ref, g_ref, bonus_ref):
    i = pl.program_id(0)
    p = p_ref[...]
    tm = p.shape[0]
    last = jnp.where(i == 0, 0.0, pprev_ref[7:8, :])
    prev = jnp.where(_iota2(p.shape, 0) == 0, last, pltpu.roll(p, 1, axis=0))
    xs = p + (prev - p) * mu_ref[...]
    r = xs[:, 0:GROUP]
    k = xs[:, GROUP:2 * GROUP]
    v = xs[:, 2 * GROUP:3 * GROUP]
    xw = xs[:, 3 * GROUP:3 * GROUP + LANES]
    xa = xs[:, 3 * GROUP + LANES:3 * GROUP + 2 * LANES]
    xg = xs[:, 3 * GROUP + 2 * LANES:3 * GROUP + 3 * LANES]
    w = -_softplus(-(w0_ref[...] + _dot(jnp.tanh(xw).astype(BF16), w2_ref[...]))) - 0.5
    logd = -jnp.exp(w)
    a = _sigmoid(a0_ref[...] + _dot(xa.astype(BF16), a2_ref[...]))
    g_ref[...] = _dot(_sigmoid(xg).astype(BF16), g2_ref[...])

    lane_head = _iota2((GROUP, GROUP), 0) // B_HD == _iota2((GROUP, GROUP), 1) // B_HD
    seg = lane_head.astype(BF16)
    kk = k * kkw_ref[...]
    nrm = jnp.sqrt(_dot_split_rhs(kk * kk, seg))
    kk = kk / jnp.maximum(nrm, 1e-12)
    k2 = k * (1.0 + (a - 1.0) * ka_ref[...])
    b = kk * a
    bonus_ref[...] = _dot_split_rhs(r * k2 * rk_ref[...], seg) * v

    row = _iota2((tm, tm), 0)
    col = _iota2((tm, tm), 1)
    same = row // B_CHUNK == col // B_CHUNK
    cl = _dot_split(jnp.logical_and(same, col <= row).astype(BF16), logd)
    cl_last = _dot_split(same.astype(BF16), logd)
    e_neg = jnp.exp(-cl)
    e_end = jnp.exp(cl_last - cl)
    at = -kk * jnp.exp(cl - logd)
    bt = b * e_neg
    kt = k2 * e_neg
    rt = r * jnp.exp(cl)
    bd = b * e_end
    kd = k2 * e_end
    for h in range(B_HEADS):
        cs = slice(h * B_HD, (h + 1) * B_HD)
        at_ref[h] = at[:, cs].astype(BF16)
        bt_ref[h] = bt[:, cs].astype(BF16)
        kt_ref[h] = kt[:, cs].astype(BF16)
        rt_ref[h] = rt[:, cs].astype(BF16)
        v_ref[h] = v[:, cs].astype(BF16)
        bd_ref[h] = bd[:, cs].astype(BF16)
        kd_ref[h] = kd[:, cs].astype(BF16)
    pc = jnp.exp(cl_last)
    for c in range(tm // B_CHUNK):
        pc_ref[c] = pc[c * B_CHUNK:c * B_CHUNK + 1, :]


def _dot_split_rhs(x, sel):
    hi, lo = _split2(x)
    return _dot(hi, sel) + _dot(lo, sel)


def _rwkv_chunk_kernel(at_ref, bt_ref, kt_ref, rt_ref, v_ref, bd_ref, kd_ref, r2_ref, y0_ref, m2_ref, sa_ref):
    c = B_CHUNK
    row = _iota2((2 * c, 2 * c), 0)
    col = _iota2((2 * c, 2 * c), 1) % c
    keep = jnp.logical_or(col < row % c, jnp.logical_and(row >= c, col == row % c))
    zeros = jnp.zeros((c, B_HD), BF16)
    nck = m2_ref.shape[0]
    units = [(j, h) for j in range(nck) for h in range(B_HEADS)]
    idx = range(len(units))

    def ld(ref, u):
        j, h = units[u]
        return ref[h, j * c:(j + 1) * c, :]

    big = [jnp.where(keep, _dot_nt(jnp.concatenate([ld(at_ref, u), ld(rt_ref, u)], axis=0),
                                   jnp.concatenate([ld(bt_ref, u), ld(kt_ref, u)], axis=0)), 0.0)
           for u in idx]
    n = [big[u][:c, :c] for u in idx]
    akv = [_dot(big[u][:c, c:].astype(BF16), ld(v_ref, u)) for u in idx]
    x = [jnp.concatenate([ld(at_ref, u).astype(F32), akv[u]], axis=1) for u in idx]
    for lvl in range(6):
        nb = [n[u].astype(BF16) for u in idx]
        x = [x[u] + _dot(nb[u], x[u].astype(BF16)) for u in idx]
        if lvl < 5:
            n = [_dot(nb[u], nb[u]) for u in idx]
    xb = [x[u].astype(BF16) for u in idx]
    ry = [_dot(big[u][c:, :].astype(BF16),
               jnp.concatenate([xb[u], jnp.concatenate([zeros, ld(v_ref, u)], axis=1)], axis=0))
          for u in idx]
    ms = [_dot_tn(xb[u], ld(bd_ref, u)) for u in idx]
    vk = [_dot_tn(ld(v_ref, u), ld(kd_ref, u)) for u in idx]
    for u in idx:
        j, h = units[u]
        rs = slice(j * c, (j + 1) * c)
        r2_ref[h, rs, :] = (ld(rt_ref, u).astype(F32) + ry[u][:, :B_HD]).astype(BF16)
        y0_ref[h, rs, :] = ry[u][:, B_HD:]
        m2_ref[j, h] = ms[u][:B_HD, :].astype(BF16)
        sa_ref[j, h] = ms[u][B_HD:, :] + vk[u]


def _rwkv_state_kernel(r2_ref, y0_ref, m2_ref, sa_ref, pc_ref, g_ref, bonus_ref, lng_ref, lnb_ref,
                       o_ref, s_ref, y_ref, *, nchunk):
    @pl.when(pl.program_id(0) == 0)
    def _():
        s_ref[...] = jnp.zeros_like(s_ref)

    c = B_CHUNK
    for j in range(nchunk):
        rs = slice(j * c, (j + 1) * c)
        pc = pc_ref[j]
        heads = range(B_HEADS)
        st = [s_ref[h] for h in heads]
        sb = [st[h].astype(BF16) for h in heads]
        sm = [_dot(sb[h], m2_ref[j, h]) for h in heads]
        ys = [_dot_nt(r2_ref[h, rs, :], sb[h]) for h in heads]
        for h in heads:
            cs = slice(h * B_HD, (h + 1) * B_HD)
            s_ref[h] = st[h] * pc[:, cs] + sm[h] + sa_ref[j, h]
            y = ys[h] + y0_ref[h, rs, :]
            ym = jnp.mean(y, axis=-1, keepdims=True)
            yc = y - ym
            yv = jnp.mean(yc * yc, axis=-1, keepdims=True)
            y_ref[rs, cs] = yc * lax.rsqrt(yv + B_LNX_EPS)
    y = y_ref[...] * lng_ref[...] + lnb_ref[...] + bonus_ref[...]
    o_ref[...] = (y * g_ref[...]).astype(o_ref.dtype)


def _rwkv(pb, mu, w0, w2, a0, a2, g2, k_k, k_a, r_k, lnx_g, lnx_b):
    s = pb.shape[0]
    tm = min(B_TM, s)
    nch = s // B_CHUNK
    vecw = pl.BlockSpec((1, B_W), lambda i: (0, 0))
    vec = pl.BlockSpec((1, GROUP), lambda i: (0, 0))
    lora = pl.BlockSpec((LANES, GROUP), lambda i: (0, 0))
    hm = pl.BlockSpec((B_HEADS, tm, B_HD), lambda i: (0, i, 0))
    full = pl.BlockSpec((tm, GROUP), lambda i: (i, 0))
    hm_shape = jax.ShapeDtypeStruct((B_HEADS, s, B_HD), BF16)
    outs = pl.pallas_call(
        _rwkv_prep_kernel,
        name="rwkv_prep",
        grid=(s // tm,),
        in_specs=[pl.BlockSpec((tm, B_W), lambda i: (i, 0)),
                  pl.BlockSpec((8, B_W), lambda i: (jnp.maximum(i * (tm // 8) - 1, 0), 0)),
                  vecw, vec, lora, vec, lora, lora, vec, vec, vec],
        out_specs=[hm] * 7 + [pl.BlockSpec((tm // B_CHUNK, 1, GROUP), lambda i: (i, 0, 0)), full, full],
        out_shape=[hm_shape] * 7 + [jax.ShapeDtypeStruct((nch, 1, GROUP), F32),
                                    jax.ShapeDtypeStruct((s, GROUP), F32),
                                    jax.ShapeDtypeStruct((s, GROUP), F32)],
        compiler_params=_params("parallel"),
    )(pb, pb, mu, w0.reshape(1, GROUP), w2, a0.reshape(1, GROUP), a2, g2,
      k_k.reshape(1, GROUP), k_a.reshape(1, GROUP), r_k.reshape(1, GROUP))
    at, bt, kt, rt, v, bd, kd, pc, g, bonus = outs

    c = B_CHUNK
    ncs = min(B_CHUNKS_PER_STEP, nch)
    hmc = pl.BlockSpec((B_HEADS, ncs * c, B_HD), lambda i: (0, i, 0))
    sq = pl.BlockSpec((ncs, B_HEADS, B_HD, B_HD), lambda i: (i, 0, 0, 0))
    r2, y0, m2, sa = pl.pallas_call(
        _rwkv_chunk_kernel,
        name="rwkv_chunk",
        grid=(nch // ncs,),
        in_specs=[hmc] * 7,
        out_specs=[hmc, hmc, sq, sq],
        out_shape=[hm_shape, jax.ShapeDtypeStruct((B_HEADS, s, B_HD), F32),
                   jax.ShapeDtypeStruct((nch, B_HEADS, B_HD, B_HD), BF16),
                   jax.ShapeDtypeStruct((nch, B_HEADS, B_HD, B_HD), F32)],
        compiler_params=_params("parallel"),
    )(at, bt, kt, rt, v, bd, kd)

    nck = min(B_STATE_CHUNKS, nch)
    rows = nck * c
    hmr = pl.BlockSpec((B_HEADS, rows, B_HD), lambda i: (0, i, 0))
    sqr = pl.BlockSpec((nck, B_HEADS, B_HD, B_HD), lambda i: (i, 0, 0, 0))
    fullr = pl.BlockSpec((rows, GROUP), lambda i: (i, 0))
    return pl.pallas_call(
        functools.partial(_rwkv_state_kernel, nchunk=nck),
        name="rwkv_state",
        grid=(nch // nck,),
        in_specs=[hmr, hmr, sqr, sqr, pl.BlockSpec((nck, 1, GROUP), lambda i: (i, 0, 0)), fullr, fullr, vec, vec],
        out_specs=fullr,
        out_shape=jax.ShapeDtypeStruct((s, GROUP), BF16),
        scratch_shapes=[pltpu.VMEM((B_HEADS, B_HD, B_HD), F32), pltpu.VMEM((rows, GROUP), F32)],
        compiler_params=_params("arbitrary"),
    )(r2, y0, m2, sa, pc, g, bonus, lnx_g.reshape(1, GROUP), lnx_b.reshape(1, GROUP))


def _moba_prep_kernel(p_ref, cos_ref, sin_ref, q_ref, k_ref, vt_ref, bias_ref, kmean_ref):
    i = pl.program_id(0)
    nb = kmean_ref.shape[0]

    @pl.when(i == 0)
    def _():
        kmean_ref[...] = jnp.zeros_like(kmean_ref)

    p = p_ref[...]
    cos = cos_ref[...]
    sin = sin_ref[...]
    first_half = (_iota2((C_BLOCK, GROUP), 1) % C_HD) < C_HD // 2

    def rope(x):
        rot = jnp.where(first_half, pltpu.roll(x, GROUP - C_HD // 2, axis=1), pltpu.roll(x, C_HD // 2, axis=1))
        return x * cos + rot * sin

    q = rope(p[:, 0:GROUP]) * (C_HD ** -0.5)
    k = rope(p[:, GROUP:2 * GROUP])
    v = p[:, 2 * GROUP:3 * GROUP]
    vt = v.T
    for h in range(C_HEADS):
        cs = slice(h * C_HD, (h + 1) * C_HD)
        q_ref[h] = (q[:, cs] * LOG2E).astype(BF16)
        k_ref[h, 0] = k[:, cs].astype(BF16)
        vt_ref[h, 0] = vt[cs, :].astype(BF16)

    km = kmean_ref[...]
    lane_h = _iota2((nb, GROUP), 1) // C_HD
    km_heads = jnp.concatenate([jnp.where(lane_h == h, km, 0.0) for h in range(C_HEADS)], axis=0)
    sc = _dot3_nt(km_heads, q).reshape(C_HEADS, nb, C_BLOCK)
    blk_i = _iota2((C_HEADS, nb, C_BLOCK), 1)
    blk = blk_i.astype(F32)
    cur = jnp.where(blk_i < i, sc, -jnp.inf)
    bias = jnp.full((C_HEADS, nb, C_BLOCK), NEG, F32)
    for _ in range(C_TOPK):
        m = jnp.max(cur, axis=1, keepdims=True)
        first = jnp.min(jnp.where(cur == m, blk, float(nb)), axis=1, keepdims=True)
        pick = jnp.logical_and(blk == first, m > -jnp.inf)
        bias = jnp.where(pick, 0.0, bias)
        cur = jnp.where(pick, -jnp.inf, cur)
    bias_ref[...] = bias
    kmean_ref[pl.ds(i, 1), :] = jnp.mean(k, axis=0, keepdims=True)


C_UNROLL = 8


def _moba_attn_kernel(q_ref, k_ref, vt_ref, bias_ref, o_ref, s_buf):
    i = pl.program_id(1)
    nb = k_ref.shape[1]
    q = q_ref[0]
    blk_a = 2 * i
    blk_b = 2 * i + 1
    slot_a = nb + C_UNROLL - 2
    slot_b = nb + C_UNROLL - 1
    shape = (C_BLOCK, 2 * C_BLOCK)
    key = _iota2(shape, 0)
    qry = _iota2(shape, 1)
    bias_a = jnp.where(qry < C_BLOCK, jnp.where(key <= qry, 0.0, NEG), bias_ref[0, pl.ds(blk_a, 1), :])
    bias_b = jnp.where(jnp.logical_and(qry >= C_BLOCK, key <= qry - C_BLOCK), 0.0, NEG)
    st_a = _dot_nt(k_ref[0, blk_a], q) + bias_a
    st_b = _dot_nt(k_ref[0, blk_b], q) + bias_b
    s_buf[slot_a] = st_a
    s_buf[slot_b] = st_b
    m = jnp.maximum(jnp.max(st_a, axis=0, keepdims=True), jnp.max(st_b, axis=0, keepdims=True))
    npast = blk_a
    ngroups = (npast + C_UNROLL - 1) // C_UNROLL

    def scores(t, m):
        js = [t * C_UNROLL + u for u in range(C_UNROLL)]
        jcs = [jnp.minimum(j, nb - 1) for j in js]
        sts = [_dot_nt(k_ref[0, jc], q) for jc in jcs]
        for j, jc, st in zip(js, jcs, sts):
            st = st + jnp.where(j < npast, bias_ref[0, pl.ds(jc, 1), :], NEG)
            s_buf[j] = st
            m = jnp.maximum(m, jnp.max(st, axis=0, keepdims=True))
        return m

    m = lax.fori_loop(0, ngroups, scores, m)

    p_a = jnp.exp2(s_buf[slot_a] - m)
    p_b = jnp.exp2(s_buf[slot_b] - m)
    l = jnp.sum(p_a, axis=0, keepdims=True) + jnp.sum(p_b, axis=0, keepdims=True)
    acc = _dot(vt_ref[0, blk_a], p_a.astype(BF16)) + _dot(vt_ref[0, blk_b], p_b.astype(BF16))

    def values(t, carry):
        l, acc = carry
        js = [t * C_UNROLL + u for u in range(C_UNROLL)]
        ps = [jnp.exp2(s_buf[j] - m) for j in js]
        pv = [_dot(vt_ref[0, jnp.minimum(j, nb - 1)], p.astype(BF16)) for j, p in zip(js, ps)]
        for p, o in zip(ps, pv):
            l = l + jnp.sum(p, axis=0, keepdims=True)
            acc = acc + o
        return l, acc

    l, acc = lax.fori_loop(0, ngroups, values, (l, acc))
    o_ref[0] = acc / l


def _moba(pc, cos, sin):
    s = pc.shape[0]
    nb = s // C_BLOCK
    tile = pl.BlockSpec((C_BLOCK, GROUP), lambda i: (i, 0))
    q, k, vt, bias = pl.pallas_call(
        _moba_prep_kernel,
        name="moba_prep",
        grid=(nb,),
        in_specs=[pl.BlockSpec((C_BLOCK, 3 * GROUP), lambda i: (i, 0)), tile, tile],
        out_specs=[pl.BlockSpec((C_HEADS, C_BLOCK, C_HD), lambda i: (0, i, 0)),
                   pl.BlockSpec((C_HEADS, 1, C_BLOCK, C_HD), lambda i: (0, i, 0, 0)),
                   pl.BlockSpec((C_HEADS, 1, C_HD, C_BLOCK), lambda i: (0, i, 0, 0)),
                   pl.BlockSpec((C_HEADS, nb, C_BLOCK), lambda i: (0, 0, i))],
        out_shape=[jax.ShapeDtypeStruct((C_HEADS, s, C_HD), BF16),
                   jax.ShapeDtypeStruct((C_HEADS, nb, C_BLOCK, C_HD), BF16),
                   jax.ShapeDtypeStruct((C_HEADS, nb, C_HD, C_BLOCK), BF16),
                   jax.ShapeDtypeStruct((C_HEADS, nb, s), F32)],
        scratch_shapes=[pltpu.VMEM((nb, GROUP), F32)],
        compiler_params=_params("arbitrary"),
    )(pc, cos, sin)
    ot = pl.pallas_call(
        _moba_attn_kernel,
        grid=(C_HEADS, nb // 2),
        in_specs=[pl.BlockSpec((1, 2 * C_BLOCK, C_HD), lambda h, i: (h, i, 0)),
                  pl.BlockSpec((1, nb, C_BLOCK, C_HD), lambda h, i: (h, 0, 0, 0)),
                  pl.BlockSpec((1, nb, C_HD, C_BLOCK), lambda h, i: (h, 0, 0, 0)),
                  pl.BlockSpec((1, nb, 2 * C_BLOCK), lambda h, i: (h, 0, i))],
        out_specs=pl.BlockSpec((1, C_HD, 2 * C_BLOCK), lambda h, i: (h, 0, i)),
        out_shape=jax.ShapeDtypeStruct((C_HEADS, C_HD, s), F32),
        scratch_shapes=[pltpu.VMEM((nb + C_UNROLL, C_BLOCK, 2 * C_BLOCK), F32)],
        compiler_params=_params("parallel", "parallel"),
        name="moba_attn",
    )(q, k, vt, bias)
    return ot.reshape(GROUP, s)


def _rope_tables(s):
    half = C_HD // 2
    inv = ROPE_THETA ** (-jnp.arange(half, dtype=F32) / half)
    ang = jnp.arange(s, dtype=F32)[:, None] * inv[None, :]
    cos = jnp.cos(ang)
    sin = jnp.sin(ang)
    cos_h = jnp.concatenate([cos, cos], axis=1)
    sin_h = jnp.concatenate([-sin, sin], axis=1)
    return jnp.tile(cos_h, (1, C_HEADS)), jnp.tile(sin_h, (1, C_HEADS))


D_W = 2 * D_HEADS * D_DK + 2 * GROUP + LANES
D_QK = D_HEADS * D_DK
D_LEVELS = 6
D_CHUNKS_PER_STEP = 4


def _gla_chunk_terms(p, w2, gb):
    c = D_CHUNK
    q = p[:, 0:D_QK] * (D_DK ** -0.5)
    k = p[:, D_QK:2 * D_QK]
    v = p[:, 2 * D_QK:2 * D_QK + GROUP]
    og = p[:, 2 * D_QK + GROUP:2 * D_QK + 2 * GROUP]
    xg = p[:, 2 * D_QK + 2 * GROUP:]
    gl = -_softplus(-(_dot(xg.astype(BF16), w2) + gb)) / D_GATE_TEMP
    row = _iota2((c, c), 0)
    col = _iota2((c, c), 1)
    cum = _dot_split((col <= row).astype(BF16), gl)
    last = cum[c - 1:c, :]

    lane_head = _iota2((c, D_QK), 1) // D_DK
    row4 = _iota2((D_HEADS * c, c), 0) % c
    col4 = _iota2((D_HEADS * c, c), 1)
    levels = range(D_LEVELS)
    blk = [c >> lvl for lvl in levels]
    rho = [_dot_split((col == (row // b) * b + (b // 2 - 1)).astype(BF16), cum) for b in blk]
    qs = [q * jnp.exp(jnp.minimum(cum - r, 0.0)) for r in rho] + [q]
    ks = [k * jnp.exp(jnp.minimum(r - cum, 0.0)) for r in rho] + [k]
    masks = [jnp.logical_and(row4 // b == col4 // b,
                             jnp.logical_and(row4 % b >= b // 2, col4 % b < b // 2)) for b in blk]
    masks.append(row4 == col4)
    heads = range(D_HEADS)
    prods = [_dot_nt(jnp.concatenate([jnp.where(lane_head == h, x, 0.0) for h in heads], axis=0).astype(BF16),
                     y.astype(BF16)) for x, y in zip(qs, ks)]
    scores = jnp.zeros((D_HEADS * c, c), F32)
    for mask, pr in zip(masks, prods):
        scores = scores + jnp.where(mask, pr, 0.0)
    sc = scores.astype(BF16)

    qe = (q * jnp.exp(cum)).astype(BF16)
    kd = k * jnp.exp(last - cum)
    e_last = jnp.exp(last)
    vh = [v[:, h * D_DV:(h + 1) * D_DV].astype(BF16) for h in heads]
    intra = [_dot(sc[h * c:(h + 1) * c, :], vh[h]) for h in heads]
    upd = [_dot_tn(vh[h], jnp.where(lane_head == h, kd, 0.0).astype(BF16)) for h in heads]
    return intra, upd, qe, e_last, og


def _gla_kernel(p_ref, w2_ref, gb_ref, ng_ref, o_ref, s_ref, *, nchunk):
    @pl.when(pl.program_id(0) == 0)
    def _():
        s_ref[...] = jnp.zeros_like(s_ref)

    c = D_CHUNK
    heads = range(D_HEADS)
    terms = [_gla_chunk_terms(p_ref[j * c:(j + 1) * c, :], w2_ref[...], gb_ref[...]) for j in range(nchunk)]
    st = [s_ref[h] for h in heads]
    for j, (intra, upd, qe, e_last, og) in enumerate(terms):
        rs = slice(j * c, (j + 1) * c)
        o = [intra[h] + _dot_nt(qe, st[h].astype(BF16)) for h in heads]
        st = [st[h] * e_last + upd[h] for h in heads]
        for h in heads:
            vs = slice(h * D_DV, (h + 1) * D_DV)
            on = o[h] * lax.rsqrt(jnp.mean(o[h] * o[h], axis=-1, keepdims=True) + EPS) * ng_ref[...]
            ogh = og[:, vs]
            o_ref[rs, vs] = (on * (ogh * _sigmoid(ogh))).astype(o_ref.dtype)
    for h in heads:
        s_ref[h] = st[h]


def _gla(pd, w2, gate_b, norm_g):
    s = pd.shape[0]
    nck = min(D_CHUNKS_PER_STEP, s // D_CHUNK)
    rows = nck * D_CHUNK
    return pl.pallas_call(
        functools.partial(_gla_kernel, nchunk=nck),
        name="gla",
        grid=(s // rows,),
        in_specs=[pl.BlockSpec((rows, D_W), lambda i: (i, 0)),
                  pl.BlockSpec((LANES, D_QK), lambda i: (0, 0)),
                  pl.BlockSpec((1, D_QK), lambda i: (0, 0)),
                  pl.BlockSpec((1, D_DV), lambda i: (0, 0))],
        out_specs=pl.BlockSpec((rows, GROUP), lambda i: (i, 0)),
        out_shape=jax.ShapeDtypeStruct((s, GROUP), BF16),
        scratch_shapes=[pltpu.VMEM((D_HEADS, D_DV, D_QK), F32)],
        compiler_params=_params("arbitrary"),
    )(pd, w2, gate_b.reshape(1, D_QK), norm_g.reshape(1, D_DV))


FFN_ROWS = 512


def _ffn_up_kernel(h_ref, wg_ref, wu_ref, cwg_ref, cwu_ref, cbg_ref, cbu_ref, o_ref, wgb, wub, gbuf, ubuf):
    tm = h_ref.shape[0]
    rows = min(tm, FFN_ROWS)

    @pl.when(pl.program_id(1) == 0)
    def _():
        wgb[...] = wg_ref[...].astype(BF16)
        wub[...] = wu_ref[...].astype(BF16)
        gbuf[0:8, :] = jnp.zeros((8, gbuf.shape[1]), F32)
        ubuf[0:8, :] = jnp.zeros((8, ubuf.shape[1]), F32)

    def conv(buf, w, cw_ref, cb_ref, r0):
        buf[8 + r0:8 + r0 + rows, :] = _dot(h_ref[r0:r0 + rows, :], w[...])
        return (cb_ref[...] + buf[6 + r0:6 + r0 + rows, :] * cw_ref[0:1, :]
                + buf[7 + r0:7 + r0 + rows, :] * cw_ref[1:2, :]
                + buf[8 + r0:8 + r0 + rows, :] * cw_ref[2:3, :])

    for r0 in range(0, tm, rows):
        gate = conv(gbuf, wgb, cwg_ref, cbg_ref, r0)
        up = conv(ubuf, wub, cwu_ref, cbu_ref, r0)
        o_ref[r0:r0 + rows, :] = (gate * _sigmoid(gate) * up).astype(o_ref.dtype)
    gbuf[0:8, :] = gbuf[tm:tm + 8, :]
    ubuf[0:8, :] = ubuf[tm:tm + 8, :]


def _ffn_up(h, w_up_all, layer, conv_w, conv_b, tm=1024, tn=512):
    m, k = h.shape
    tm = min(tm, m)
    nj = D_FF // tn
    cb = conv_b.reshape(1, 2 * D_FF)
    return pl.pallas_call(
        _ffn_up_kernel,
        name="ffn_up",
        grid=(nj, m // tm),
        in_specs=[pl.BlockSpec((tm, k), lambda j, i: (i, 0)),
                  pl.BlockSpec((None, k, tn), lambda j, i: (layer, 0, j)),
                  pl.BlockSpec((None, k, tn), lambda j, i: (layer, 0, j + nj)),
                  pl.BlockSpec((3, tn), lambda j, i: (0, j)),
                  pl.BlockSpec((3, tn), lambda j, i: (0, j + nj)),
                  pl.BlockSpec((1, tn), lambda j, i: (0, j)),
                  pl.BlockSpec((1, tn), lambda j, i: (0, j + nj))],
        out_specs=pl.BlockSpec((tm, tn), lambda j, i: (i, j)),
        out_shape=jax.ShapeDtypeStruct((m, D_FF), BF16),
        scratch_shapes=[pltpu.VMEM((k, tn), BF16), pltpu.VMEM((k, tn), BF16),
                        pltpu.VMEM((tm + 8, tn), F32), pltpu.VMEM((tm + 8, tn), F32)],
        compiler_params=_params("parallel", "arbitrary"),
    )(h, w_up_all, w_up_all, conv_w, conv_w, cb, cb)


def _pad_cols(w, width):
    return jnp.pad(w, ((0, 0), (0, width - w.shape[1])))


def _pad_rows(w, height):
    return jnp.pad(w, ((0, height - w.shape[0]), (0, 0)))


def _split_w_in(w):
    a_in = 2 * GROUP
    b0 = a_in
    wa = w[:, :a_in]
    rkv = w[:, b0:b0 + 3 * GROUP]
    xw = w[:, b0 + 3 * GROUP:b0 + 3 * GROUP + 32]
    xa = w[:, b0 + 3 * GROUP + 32:b0 + 3 * GROUP + 64]
    xg = w[:, b0 + 3 * GROUP + 64:b0 + 3 * GROUP + 160]
    wb = jnp.concatenate([rkv, _pad_cols(xw, LANES), _pad_cols(xa, LANES), _pad_cols(xg, LANES)], axis=1)
    c0 = b0 + 3 * GROUP + 160
    wc = w[:, c0:c0 + 3 * GROUP]
    d0 = c0 + 3 * GROUP
    qkv = w[:, d0:d0 + 2 * D_QK + GROUP]
    dxg = w[:, d0 + 2 * D_QK + GROUP:d0 + 2 * D_QK + GROUP + 16]
    og = w[:, d0 + 2 * D_QK + GROUP + 16:d0 + 2 * D_QK + 2 * GROUP + 16]
    wd = jnp.concatenate([qkv, og, _pad_cols(dxg, LANES)], axis=1)
    return [t.astype(BF16) for t in (wa, wb, wc, wd)]


def _split_mu(mu):
    rkv = mu[:3 * GROUP]
    pads = [jnp.pad(mu[3 * GROUP + lo:3 * GROUP + hi], (0, LANES - (hi - lo))) for lo, hi in ((0, 32), (32, 64), (64, 160))]
    return jnp.concatenate([rkv] + pads).reshape(1, B_W)


def kernel(x, mix_norm_g, w_in, a_ln_g, a_ln_b, a_ws, a_bs, b_mu, b_w0, b_w2, b_a0, b_a2, b_g2, b_kk, b_ka, b_rk, b_lnx_g, b_lnx_b, d_gate_w2, d_gate_b, d_norm_g, w_out, ffn_norm_g, w_up, conv_w, conv_b, w_down, final_norm_g):
    bsz, s, d = x.shape
    depth = w_in.shape[0]
    cos, sin = _rope_tables(s)
    outs = []
    for bi in range(bsz):
        xc = x[bi]
        for l in range(depth):
            h = _rmsnorm(xc, mix_norm_g[l], BF16)
            wa, wb, wc, wd = _split_w_in(w_in[l])
            pa = _matmul(h, wa, F32)
            pb = _matmul(h, wb, F32)
            pc = _matmul(h, wc, F32)
            pd = _matmul(h, wd, F32)
            ya = _gmlp(pa, a_ln_g[l], a_ln_b[l], a_ws[l], a_bs[l])
            yb = _rwkv(pb, _split_mu(b_mu[l]), b_w0[l], _pad_rows(b_w2[l], LANES).astype(BF16), b_a0[l],
                       _pad_rows(b_a2[l], LANES).astype(BF16), _pad_rows(b_g2[l], LANES).astype(BF16),
                       b_kk[l], b_ka[l], b_rk[l], b_lnx_g[l], b_lnx_b[l])
            yct = _moba(pc, cos, sin)
            yd = _gla(pd, _pad_rows(d_gate_w2[l], LANES).astype(BF16), d_gate_b[l], d_norm_g[l])
            xc, h2 = _outproj(ya, yb, yct, yd, w_out[l].astype(BF16), xc, ffn_norm_g[l])
            act = _ffn_up(h2, w_up, l, conv_w[l], conv_b[l])
            xc = _matmul_res(act, w_down, l, xc)
        outs.append(_rmsnorm(xc, final_norm_g, F32))
    return jnp.stack(outs, axis=0)
```

```python
import functools
import math

import jax
import jax.numpy as jnp
import numpy as np
from jax import lax
from jax.experimental import pallas as pl
from jax.experimental.pallas import tpu as pltpu

F32 = jnp.float32
BF16 = jnp.bfloat16

EPS = 1e-6
GROUP = 512
LANES = 128
VMEM_LIMIT = 56 * 1024 * 1024

A_HEADS, A_CH, A_CHUNK = 4, 128, 128
B_HEADS, B_HD, B_CHUNK = 8, 64, 64
B_LNX_EPS = 64e-5
C_HEADS, C_HD, C_BLOCK, C_TOPK = 8, 64, 256, 3
ROPE_THETA = 10000.0
D_HEADS, D_DK, D_DV, D_CHUNK = 4, 64, 128, 64
D_GATE_TEMP = 16.0
D_FF = 5632
NEG = -1e30
LOG2E = 1.4426950408889634


def _params(*sem):
    return pltpu.CompilerParams(dimension_semantics=sem, vmem_limit_bytes=VMEM_LIMIT)


def _dot(a, b):
    return jnp.dot(a, b, preferred_element_type=F32)


def _dot_nt(a, b):
    return lax.dot_general(a, b, (((1,), (1,)), ((), ())), preferred_element_type=F32)


def _dot_tn(a, b):
    return lax.dot_general(a, b, (((0,), (0,)), ((), ())), preferred_element_type=F32)


def _split3(x):
    hi = x.astype(BF16)
    r1 = x - hi.astype(F32)
    mid = r1.astype(BF16)
    lo = (r1 - mid.astype(F32)).astype(BF16)
    return hi, mid, lo


def _split2(x):
    hi = x.astype(BF16)
    return hi, (x - hi.astype(F32)).astype(BF16)


def _dot_split(sel, x):
    hi, lo = _split2(x)
    return _dot(sel, hi) + _dot(sel, lo)


def _dot3_nt(a, b):
    ah, am, al = _split3(a)
    bh, bm, bl = _split3(b)
    return (_dot_nt(ah, bh) + (_dot_nt(ah, bm) + _dot_nt(am, bh))
            + (_dot_nt(am, bm) + _dot_nt(ah, bl) + _dot_nt(al, bh)))


def _sigmoid(x):
    return 1.0 / (1.0 + jnp.exp(-x))


def _softplus(x):
    return jnp.maximum(x, 0.0) + jnp.log(1.0 + jnp.exp(-jnp.abs(x)))


def _iota2(shape, axis):
    return lax.broadcasted_iota(jnp.int32, shape, axis)


def _rmsnorm_kernel(x_ref, g_ref, o_ref):
    x = x_ref[...]
    ms = jnp.mean(x * x, axis=-1, keepdims=True)
    o_ref[...] = (x * lax.rsqrt(ms + EPS) * g_ref[...]).astype(o_ref.dtype)


def _rmsnorm(x, g, out_dtype, tm=512):
    m, d = x.shape
    return pl.pallas_call(
        _rmsnorm_kernel,
        name="rmsnorm",
        grid=(m // tm,),
        in_specs=[pl.BlockSpec((tm, d), lambda i: (i, 0)),
                  pl.BlockSpec((1, d), lambda i: (0, 0))],
        out_specs=pl.BlockSpec((tm, d), lambda i: (i, 0)),
        out_shape=jax.ShapeDtypeStruct((m, d), out_dtype),
        compiler_params=_params("parallel"),
    )(x, g.reshape(1, d))


def _mm_kernel(x_ref, w_ref, o_ref):
    o_ref[...] = _dot(x_ref[...], w_ref[...]).astype(o_ref.dtype)


def _matmul(x, w, out_dtype, tm=512):
    m, k = x.shape
    n = w.shape[1]
    return pl.pallas_call(
        _mm_kernel,
        name="inproj",
        grid=(m // tm,),
        in_specs=[pl.BlockSpec((tm, k), lambda i: (i, 0)),
                  pl.BlockSpec((k, n), lambda i: (0, 0))],
        out_specs=pl.BlockSpec((tm, n), lambda i: (i, 0)),
        out_shape=jax.ShapeDtypeStruct((m, n), out_dtype),
        compiler_params=_params("parallel"),
    )(x, w)


def _mm_res_kernel(x_ref, w_ref, r_ref, o_ref, wb):
    @pl.when(pl.program_id(1) == 0)
    def _():
        wb[...] = w_ref[...].astype(BF16)

    o_ref[...] = r_ref[...] + _dot(x_ref[...], wb[...])


def _matmul_res(x, w_all, layer, res, tm=512, tn=512):
    m, k = x.shape
    n = w_all.shape[2]
    tm = min(tm, m)
    return pl.pallas_call(
        _mm_res_kernel,
        name="ffn_down",
        grid=(n // tn, m // tm),
        in_specs=[pl.BlockSpec((tm, k), lambda j, i: (i, 0)),
                  pl.BlockSpec((None, k, tn), lambda j, i: (layer, 0, j)),
                  pl.BlockSpec((tm, tn), lambda j, i: (i, j))],
        out_specs=pl.BlockSpec((tm, tn), lambda j, i: (i, j)),
        out_shape=jax.ShapeDtypeStruct((m, n), F32),
        scratch_shapes=[pltpu.VMEM((k, tn), BF16)],
        compiler_params=_params("parallel", "arbitrary"),
    )(x, w_all, res)


def _outproj_kernel(ya_ref, yb_ref, yct_ref, yd_ref, w_ref, x_ref, g_ref, o_ref, h_ref):
    yc = yct_ref[...].T.astype(BF16)
    acc = _dot(ya_ref[...], w_ref[0:GROUP, :])
    acc += _dot(yb_ref[...], w_ref[GROUP:2 * GROUP, :])
    acc += _dot(yc, w_ref[2 * GROUP:3 * GROUP, :])
    acc += _dot(yd_ref[...], w_ref[3 * GROUP:4 * GROUP, :])
    x = x_ref[...] + acc
    o_ref[...] = x
    ms = jnp.mean(x * x, axis=-1, keepdims=True)
    h_ref[...] = (x * lax.rsqrt(ms + EPS) * g_ref[...]).astype(h_ref.dtype)


def _outproj(ya, yb, yct, yd, w, x, g, tm=256):
    m, n = x.shape
    row = pl.BlockSpec((tm, GROUP), lambda i: (i, 0))
    full = pl.BlockSpec((tm, n), lambda i: (i, 0))
    return pl.pallas_call(
        _outproj_kernel,
        name="outproj",
        grid=(m // tm,),
        in_specs=[row, row,
                  pl.BlockSpec((GROUP, tm), lambda i: (0, i)),
                  row,
                  pl.BlockSpec((4 * GROUP, n), lambda i: (0, 0)),
                  full,
                  pl.BlockSpec((1, n), lambda i: (0, 0))],
        out_specs=[full, full],
        out_shape=[jax.ShapeDtypeStruct((m, n), F32), jax.ShapeDtypeStruct((m, n), BF16)],
        compiler_params=_params("parallel"),
    )(ya, yb, yct, yd, w, x, g.reshape(1, n))


def _gmlp_kernel(p_ref, lng_ref, lnb_ref, ws_ref, bsb_ref, o_ref, *, nchunk):
    p = p_ref[...]
    z = 0.5 * p * (1.0 + jnp.tanh(math.sqrt(2.0 / math.pi) * (p + 0.044715 * (p * p * p))))
    u = z[:, :GROUP]
    v = z[:, GROUP:]
    mu = jnp.mean(v, axis=-1, keepdims=True)
    vc = v - mu
    var = jnp.mean(vc * vc, axis=-1, keepdims=True)
    vn = (vc * lax.rsqrt(var + EPS) * lng_ref[...] + lnb_ref[...]).astype(BF16)
    tri = _iota2((A_CHUNK, A_CHUNK), 1) <= _iota2((A_CHUNK, A_CHUNK), 0)
    for h in range(A_HEADS):
        w = jnp.where(tri, ws_ref[h], 0.0).astype(BF16)
        cs = slice(h * A_CH, (h + 1) * A_CH)
        for c in range(nchunk):
            rs = slice(c * A_CHUNK, (c + 1) * A_CHUNK)
            mixed = _dot(w, vn[rs, cs]) + bsb_ref[:, cs]
            o_ref[rs, cs] = (u[rs, cs] * mixed).astype(o_ref.dtype)


def _gmlp(pa, ln_g, ln_b, ws, bs, nchunk=4):
    s = pa.shape[0]
    tm = nchunk * A_CHUNK
    bsb = jnp.repeat(bs.T, A_CH, axis=1)
    vec = pl.BlockSpec((1, GROUP), lambda i: (0, 0))
    return pl.pallas_call(
        functools.partial(_gmlp_kernel, nchunk=nchunk),
        name="gmlp",
        grid=(s // tm,),
        in_specs=[pl.BlockSpec((tm, 2 * GROUP), lambda i: (i, 0)), vec, vec,
                  pl.BlockSpec((A_HEADS, A_CHUNK, A_CHUNK), lambda i: (0, 0, 0)),
                  pl.BlockSpec((A_CHUNK, GROUP), lambda i: (0, 0))],
        out_specs=pl.BlockSpec((tm, GROUP), lambda i: (i, 0)),
        out_shape=jax.ShapeDtypeStruct((s, GROUP), BF16),
        compiler_params=_params("parallel"),
    )(pa, ln_g.reshape(1, GROUP), ln_b.reshape(1, GROUP), ws, bsb)


B_W = 3 * GROUP + 3 * LANES
B_TM = 512
B_STATE_CHUNKS = 4
B_CHUNKS_PER_STEP = 4


def _rwkv_prep_kernel(p_ref, pprev_ref, mu_ref, w0_ref, w2_ref, a0_ref, a2_ref, g2_ref,
                      kkw_ref, ka_ref, rk_ref,
                      at_ref, bt_ref, kt_ref, rt_ref, v_ref, bd_ref, kd_ref, pc_ref, g_ref, bonus_ref):
    i = pl.program_id(0)
    p = p_ref[...]
    tm = p.shape[0]
    last = jnp.where(i == 0, 0.0, pprev_ref[7:8, :])
    prev = jnp.where(_iota2(p.shape, 0) == 0, last, pltpu.roll(p, 1, axis=0))
    xs = p + (prev - p) * mu_ref[...]
    r = xs[:, 0:GROUP]
    k = xs[:, GROUP:2 * GROUP]
    v = xs[:, 2 * GROUP:3 * GROUP]
    xw = xs[:, 3 * GROUP:3 * GROUP + LANES]
    xa = xs[:, 3 * GROUP + LANES:3 * GROUP + 2 * LANES]
    xg = xs[:, 3 * GROUP + 2 * LANES:3 * GROUP + 3 * LANES]
    w = -_softplus(-(w0_ref[...] + _dot(jnp.tanh(xw).astype(BF16), w2_ref[...]))) - 0.5
    logd = -jnp.exp(w)
    a = _sigmoid(a0_ref[...] + _dot(xa.astype(BF16), a2_ref[...]))
    g_ref[...] = _dot(_sigmoid(xg).astype(BF16), g2_ref[...])

    lane_head = _iota2((GROUP, GROUP), 0) // B_HD == _iota2((GROUP, GROUP), 1) // B_HD
    seg = lane_head.astype(BF16)
    kk = k * kkw_ref[...]
    nrm = jnp.sqrt(_dot_split_rhs(kk * kk, seg))
    kk = kk / jnp.maximum(nrm, 1e-12)
    k2 = k * (1.0 + (a - 1.0) * ka_ref[...])
    b = kk * a
    bonus_ref[...] = _dot_split_rhs(r * k2 * rk_ref[...], seg) * v

    row = _iota2((tm, tm), 0)
    col = _iota2((tm, tm), 1)
    same = row // B_CHUNK == col // B_CHUNK
    cl = _dot_split(jnp.logical_and(same, col <= row).astype(BF16), logd)
    cl_last = _dot_split(same.astype(BF16), logd)
    e_neg = jnp.exp(-cl)
    e_end = jnp.exp(cl_last - cl)
    at = -kk * jnp.exp(cl - logd)
    bt = b * e_neg
    kt = k2 * e_neg
    rt = r * jnp.exp(cl)
    bd = b * e_end
    kd = k2 * e_end
    for h in range(B_HEADS):
        cs = slice(h * B_HD, (h + 1) * B_HD)
        at_ref[h] = at[:, cs].astype(BF16)
        bt_ref[h] = bt[:, cs].astype(BF16)
        kt_ref[h] = kt[:, cs].astype(BF16)
        rt_ref[h] = rt[:, cs].astype(BF16)
        v_ref[h] = v[:, cs].astype(BF16)
        bd_ref[h] = bd[:, cs].astype(BF16)
        kd_ref[h] = kd[:, cs].astype(BF16)
    pc = jnp.exp(cl_last)
    for c in range(tm // B_CHUNK):
        pc_ref[c] = pc[c * B_CHUNK:c * B_CHUNK + 1, :]


def _dot_split_rhs(x, sel):
    hi, lo = _split2(x)
    return _dot(hi, sel) + _dot(lo, sel)


def _rwkv_chunk_kernel(at_ref, bt_ref, kt_ref, rt_ref, v_ref, bd_ref, kd_ref, r2_ref, y0_ref, m2_ref, sa_ref):
    c = B_CHUNK
    row = _iota2((2 * c, 2 * c), 0)
    col = _iota2((2 * c, 2 * c), 1) % c
    keep = jnp.logical_or(col < row % c, jnp.logical_and(row >= c, col == row % c))
    zeros = jnp.zeros((c, B_HD), BF16)
    nck = m2_ref.shape[0]
    units = [(j, h) for j in range(nck) for h in range(B_HEADS)]
    idx = range(len(units))

    def ld(ref, u):
        j, h = units[u]
        return ref[h, j * c:(j + 1) * c, :]

    big = [jnp.where(keep, _dot_nt(jnp.concatenate([ld(at_ref, u), ld(rt_ref, u)], axis=0),
                                   jnp.concatenate([ld(bt_ref, u), ld(kt_ref, u)], axis=0)), 0.0)
           for u in idx]
    n = [big[u][:c, :c] for u in idx]
    akv = [_dot(big[u][:c, c:].astype(BF16), ld(v_ref, u)) for u in idx]
    x = [jnp.concatenate([ld(at_ref, u).astype(F32), akv[u]], axis=1) for u in idx]
    for lvl in range(6):
        nb = [n[u].astype(BF16) for u in idx]
        x = [x[u] + _dot(nb[u], x[u].astype(BF16)) for u in idx]
        if lvl < 5:
            n = [_dot(nb[u], nb[u]) for u in idx]
    xb = [x[u].astype(BF16) for u in idx]
    ry = [_dot(big[u][c:, :].astype(BF16),
               jnp.concatenate([xb[u], jnp.concatenate([zeros, ld(v_ref, u)], axis=1)], axis=0))
          for u in idx]
    ms = [_dot_tn(xb[u], ld(bd_ref, u)) for u in idx]
    vk = [_dot_tn(ld(v_ref, u), ld(kd_ref, u)) for u in idx]
    for u in idx:
        j, h = units[u]
        rs = slice(j * c, (j + 1) * c)
        r2_ref[h, rs, :] = (ld(rt_ref, u).astype(F32) + ry[u][:, :B_HD]).astype(BF16)
        y0_ref[h, rs, :] = ry[u][:, B_HD:]
        m2_ref[j, h] = ms[u][:B_HD, :].astype(BF16)
        sa_ref[j, h] = ms[u][B_HD:, :] + vk[u]


def _rwkv_state_kernel(r2_ref, y0_ref, m2_ref, sa_ref, pc_ref, g_ref, bonus_ref, lng_ref, lnb_ref,
                       o_ref, s_ref, y_ref, *, nchunk):
    @pl.when(pl.program_id(0) == 0)
    def _():
        s_ref[...] = jnp.zeros_like(s_ref)

    c = B_CHUNK
    for j in range(nchunk):
        rs = slice(j * c, (j + 1) * c)
        pc = pc_ref[j]
        heads = range(B_HEADS)
        st = [s_ref[h] for h in heads]
        sb = [st[h].astype(BF16) for h in heads]
        sm = [_dot(sb[h], m2_ref[j, h]) for h in heads]
        ys = [_dot_nt(r2_ref[h, rs, :], sb[h]) for h in heads]
        for h in heads:
            cs = slice(h * B_HD, (h + 1) * B_HD)
            s_ref[h] = st[h] * pc[:, cs] + sm[h] + sa_ref[j, h]
            y = ys[h] + y0_ref[h, rs, :]
            ym = jnp.mean(y, axis=-1, keepdims=True)
            yc = y - ym
            yv = jnp.mean(yc * yc, axis=-1, keepdims=True)
            y_ref[rs, cs] = yc * lax.rsqrt(yv + B_LNX_EPS)
    y = y_ref[...] * lng_ref[...] + lnb_ref[...] + bonus_ref[...]
    o_ref[...] = (y * g_ref[...]).astype(o_ref.dtype)


def _rwkv(pb, mu, w0, w2, a0, a2, g2, k_k, k_a, r_k, lnx_g, lnx_b):
    s = pb.shape[0]
    tm = min(B_TM, s)
    nch = s // B_CHUNK
    vecw = pl.BlockSpec((1, B_W), lambda i: (0, 0))
    vec = pl.BlockSpec((1, GROUP), lambda i: (0, 0))
    lora = pl.BlockSpec((LANES, GROUP), lambda i: (0, 0))
    hm = pl.BlockSpec((B_HEADS, tm, B_HD), lambda i: (0, i, 0))
    full = pl.BlockSpec((tm, GROUP), lambda i: (i, 0))
    hm_shape = jax.ShapeDtypeStruct((B_HEADS, s, B_HD), BF16)
    outs = pl.pallas_call(
        _rwkv_prep_kernel,
        name="rwkv_prep",
        grid=(s // tm,),
        in_specs=[pl.BlockSpec((tm, B_W), lambda i: (i, 0)),
                  pl.BlockSpec((8, B_W), lambda i: (jnp.maximum(i * (tm // 8) - 1, 0), 0)),
                  vecw, vec, lora, vec, lora, lora, vec, vec, vec],
        out_specs=[hm] * 7 + [pl.BlockSpec((tm // B_CHUNK, 1, GROUP), lambda i: (i, 0, 0)), full, full],
        out_shape=[hm_shape] * 7 + [jax.ShapeDtypeStruct((nch, 1, GROUP), F32),
                                    jax.ShapeDtypeStruct((s, GROUP), F32),
                                    jax.ShapeDtypeStruct((s, GROUP), F32)],
        compiler_params=_params("parallel"),
    )(pb, pb, mu, w0.reshape(1, GROUP), w2, a0.reshape(1, GROUP), a2, g2,
      k_k.reshape(1, GROUP), k_a.reshape(1, GROUP), r_k.reshape(1, GROUP))
    at, bt, kt, rt, v, bd, kd, pc, g, bonus = outs

    c = B_CHUNK
    ncs = min(B_CHUNKS_PER_STEP, nch)
    hmc = pl.BlockSpec((B_HEADS, ncs * c, B_HD), lambda i: (0, i, 0))
    sq = pl.BlockSpec((ncs, B_HEADS, B_HD, B_HD), lambda i: (i, 0, 0, 0))
    r2, y0, m2, sa = pl.pallas_call(
        _rwkv_chunk_kernel,
        name="rwkv_chunk",
        grid=(nch // ncs,),
        in_specs=[hmc] * 7,
        out_specs=[hmc, hmc, sq, sq],
        out_shape=[hm_shape, jax.ShapeDtypeStruct((B_HEADS, s, B_HD), F32),
                   jax.ShapeDtypeStruct((nch, B_HEADS, B_HD, B_HD), BF16),
                   jax.ShapeDtypeStruct((nch, B_HEADS, B_HD, B_HD), F32)],
        compiler_params=_params("parallel"),
    )(at, bt, kt, rt, v, bd, kd)

    nck = min(B_STATE_CHUNKS, nch)
    rows = nck * c
    hmr = pl.BlockSpec((B_HEADS, rows, B_HD), lambda i: (0, i, 0))
    sqr = pl.BlockSpec((nck, B_HEADS, B_HD, B_HD), lambda i: (i, 0, 0, 0))
    fullr = pl.BlockSpec((rows, GROUP), lambda i: (i, 0))
    return pl.pallas_call(
        functools.partial(_rwkv_state_kernel, nchunk=nck),
        name="rwkv_state",
        grid=(nch // nck,),
        in_specs=[hmr, hmr, sqr, sqr, pl.BlockSpec((nck, 1, GROUP), lambda i: (i, 0, 0)), fullr, fullr, vec, vec],
        out_specs=fullr,
        out_shape=jax.ShapeDtypeStruct((s, GROUP), BF16),
        scratch_shapes=[pltpu.VMEM((B_HEADS, B_HD, B_HD), F32), pltpu.VMEM((rows, GROUP), F32)],
        compiler_params=_params("arbitrary"),
    )(r2, y0, m2, sa, pc, g, bonus, lnx_g.reshape(1, GROUP), lnx_b.reshape(1, GROUP))


def _moba_prep_kernel(p_ref, cos_ref, sin_ref, q_ref, k_ref, vt_ref, bias_ref, kmean_ref):
    i = pl.program_id(0)
    nb = kmean_ref.shape[0]

    @pl.when(i == 0)
    def _():
        kmean_ref[...] = jnp.zeros_like(kmean_ref)

    p = p_ref[...]
    cos = cos_ref[...]
    sin = sin_ref[...]
    first_half = (_iota2((C_BLOCK, GROUP), 1) % C_HD) < C_HD // 2

    def rope(x):
        rot = jnp.where(first_half, pltpu.roll(x, GROUP - C_HD // 2, axis=1), pltpu.roll(x, C_HD // 2, axis=1))
        return x * cos + rot * sin

    q = rope(p[:, 0:GROUP]) * (C_HD ** -0.5)
    k = rope(p[:, GROUP:2 * GROUP])
    v = p[:, 2 * GROUP:3 * GROUP]
    vt = v.T
    for h in range(C_HEADS):
        cs = slice(h * C_HD, (h + 1) * C_HD)
        q_ref[h] = (q[:, cs] * LOG2E).astype(BF16)
        k_ref[h, 0] = k[:, cs].astype(BF16)
        vt_ref[h, 0] = vt[cs, :].astype(BF16)

    km = kmean_ref[...]
    lane_h = _iota2((nb, GROUP), 1) // C_HD
    km_heads = jnp.concatenate([jnp.where(lane_h == h, km, 0.0) for h in range(C_HEADS)], axis=0)
    sc = _dot3_nt(km_heads, q).reshape(C_HEADS, nb, C_BLOCK)
    blk_i = _iota2((C_HEADS, nb, C_BLOCK), 1)
    blk = blk_i.astype(F32)
    cur = jnp.where(blk_i < i, sc, -jnp.inf)
    bias = jnp.full((C_HEADS, nb, C_BLOCK), NEG, F32)
    for _ in range(C_TOPK):
        m = jnp.max(cur, axis=1, keepdims=True)
        first = jnp.min(jnp.where(cur == m, blk, float(nb)), axis=1, keepdims=True)
        pick = jnp.logical_and(blk == first, m > -jnp.inf)
        bias = jnp.where(pick, 0.0, bias)
        cur = jnp.where(pick, -jnp.inf, cur)
    bias_ref[...] = bias
    kmean_ref[pl.ds(i, 1), :] = jnp.mean(k, axis=0, keepdims=True)


C_UNROLL = 8


def _moba_attn_kernel(q_ref, k_ref, vt_ref, bias_ref, o_ref, s_buf):
    i = pl.program_id(1)
    nb = k_ref.shape[1]
    q = q_ref[0]
    blk_a = 2 * i
    blk_b = 2 * i + 1
    slot_a = nb + C_UNROLL - 2
    slot_b = nb + C_UNROLL - 1
    shape = (C_BLOCK, 2 * C_BLOCK)
    key = _iota2(shape, 0)
    qry = _iota2(shape, 1)
    bias_a = jnp.where(qry < C_BLOCK, jnp.where(key <= qry, 0.0, NEG), bias_ref[0, pl.ds(blk_a, 1), :])
    bias_b = jnp.where(jnp.logical_and(qry >= C_BLOCK, key <= qry - C_BLOCK), 0.0, NEG)
    st_a = _dot_nt(k_ref[0, blk_a], q) + bias_a
    st_b = _dot_nt(k_ref[0, blk_b], q) + bias_b
    s_buf[slot_a] = st_a
    s_buf[slot_b] = st_b
    m = jnp.maximum(jnp.max(st_a, axis=0, keepdims=True), jnp.max(st_b, axis=0, keepdims=True))
    npast = blk_a
    ngroups = (npast + C_UNROLL - 1) // C_UNROLL

    def scores(t, m):
        js = [t * C_UNROLL + u for u in range(C_UNROLL)]
        jcs = [jnp.minimum(j, nb - 1) for j in js]
        sts = [_dot_nt(k_ref[0, jc], q) for jc in jcs]
        for j, jc, st in zip(js, jcs, sts):
            st = st + jnp.where(j < npast, bias_ref[0, pl.ds(jc, 1), :], NEG)
            s_buf[j] = st
            m = jnp.maximum(m, jnp.max(st, axis=0, keepdims=True))
        return m

    m = lax.fori_loop(0, ngroups, scores, m)

    p_a = jnp.exp2(s_buf[slot_a] - m)
    p_b = jnp.exp2(s_buf[slot_b] - m)
    l = jnp.sum(p_a, axis=0, keepdims=True) + jnp.sum(p_b, axis=0, keepdims=True)
    acc = _dot(vt_ref[0, blk_a], p_a.astype(BF16)) + _dot(vt_ref[0, blk_b], p_b.astype(BF16))

    def values(t, carry):
        l, acc = carry
        js = [t * C_UNROLL + u for u in range(C_UNROLL)]
        ps = [jnp.exp2(s_buf[j] - m) for j in js]
        pv = [_dot(vt_ref[0, jnp.minimum(j, nb - 1)], p.astype(BF16)) for j, p in zip(js, ps)]
        for p, o in zip(ps, pv):
            l = l + jnp.sum(p, axis=0, keepdims=True)
            acc = acc + o
        return l, acc

    l, acc = lax.fori_loop(0, ngroups, values, (l, acc))
    o_ref[0] = acc / l


def _moba(pc, cos, sin):
    s = pc.shape[0]
    nb = s // C_BLOCK
    tile = pl.BlockSpec((C_BLOCK, GROUP), lambda i: (i, 0))
    q, k, vt, bias = pl.pallas_call(
        _moba_prep_kernel,
        name="moba_prep",
        grid=(nb,),
        in_specs=[pl.BlockSpec((C_BLOCK, 3 * GROUP), lambda i: (i, 0)), tile, tile],
        out_specs=[pl.BlockSpec((C_HEADS, C_BLOCK, C_HD), lambda i: (0, i, 0)),
                   pl.BlockSpec((C_HEADS, 1, C_BLOCK, C_HD), lambda i: (0, i, 0, 0)),
                   pl.BlockSpec((C_HEADS, 1, C_HD, C_BLOCK), lambda i: (0, i, 0, 0)),
                   pl.BlockSpec((C_HEADS, nb, C_BLOCK), lambda i: (0, 0, i))],
        out_shape=[jax.ShapeDtypeStruct((C_HEADS, s, C_HD), BF16),
                   jax.ShapeDtypeStruct((C_HEADS, nb, C_BLOCK, C_HD), BF16),
                   jax.ShapeDtypeStruct((C_HEADS, nb, C_HD, C_BLOCK), BF16),
                   jax.ShapeDtypeStruct((C_HEADS, nb, s), F32)],
        scratch_shapes=[pltpu.VMEM((nb, GROUP), F32)],
        compiler_params=_params("arbitrary"),
    )(pc, cos, sin)
    ot = pl.pallas_call(
        _moba_attn_kernel,
        grid=(C_HEADS, nb // 2),
        in_specs=[pl.BlockSpec((1, 2 * C_BLOCK, C_HD), lambda h, i: (h, i, 0)),
                  pl.BlockSpec((1, nb, C_BLOCK, C_HD), lambda h, i: (h, 0, 0, 0)),
                  pl.BlockSpec((1, nb, C_HD, C_BLOCK), lambda h, i: (h, 0, 0, 0)),
                  pl.BlockSpec((1, nb, 2 * C_BLOCK), lambda h, i: (h, 0, i))],
        out_specs=pl.BlockSpec((1, C_HD, 2 * C_BLOCK), lambda h, i: (h, 0, i)),
        out_shape=jax.ShapeDtypeStruct((C_HEADS, C_HD, s), F32),
        scratch_shapes=[pltpu.VMEM((nb + C_UNROLL, C_BLOCK, 2 * C_BLOCK), F32)],
        compiler_params=_params("parallel", "parallel"),
        name="moba_attn",
    )(q, k, vt, bias)
    return ot.reshape(GROUP, s)


def _rope_tables(s):
    half = C_HD // 2
    inv = ROPE_THETA ** (-jnp.arange(half, dtype=F32) / half)
    ang = jnp.arange(s, dtype=F32)[:, None] * inv[None, :]
    cos = jnp.cos(ang)
    sin = jnp.sin(ang)
    cos_h = jnp.concatenate([cos, cos], axis=1)
    sin_h = jnp.concatenate([-sin, sin], axis=1)
    return jnp.tile(cos_h, (1, C_HEADS)), jnp.tile(sin_h, (1, C_HEADS))


D_W = 2 * D_HEADS * D_DK + 2 * GROUP + LANES
D_QK = D_HEADS * D_DK
D_LEVELS = 6
D_CHUNKS_PER_STEP = 4


def _gla_chunk_terms(p, w2, gb, nchunk):
    c = D_CHUNK
    n = nchunk * c
    q = p[:, 0:D_QK] * (D_DK ** -0.5)
    k = p[:, D_QK:2 * D_QK]
    v = p[:, 2 * D_QK:2 * D_QK + GROUP]
    og = p[:, 2 * D_QK + GROUP:2 * D_QK + 2 * GROUP]
    xg = p[:, 2 * D_QK + 2 * GROUP:]
    gl = -_softplus(-(_dot(xg.astype(BF16), w2) + gb)) / D_GATE_TEMP
    row = _iota2((n, n), 0)
    col = _iota2((n, n), 1)
    same = row // c == col // c
    cum = _dot_split(jnp.logical_and(same, col <= row).astype(BF16), gl)
    last = _dot_split((col == (row // c) * c + (c - 1)).astype(BF16), cum)

    lane_head = _iota2((c, D_QK), 1) // D_DK
    row4 = _iota2((D_HEADS * c, c), 0) % c
    col4 = _iota2((D_HEADS * c, c), 1)
    levels = range(D_LEVELS)
    blk = [c >> lvl for lvl in levels]
    rho = [_dot_split((col == (row // b) * b + (b // 2 - 1)).astype(BF16), cum) for b in blk]
    qs = [q * jnp.exp(jnp.minimum(cum - r, 0.0)) for r in rho] + [q]
    ks = [k * jnp.exp(jnp.minimum(r - cum, 0.0)) for r in rho] + [k]
    masks = [jnp.logical_and(row4 // b == col4 // b,
                             jnp.logical_and(row4 % b >= b // 2, col4 % b < b // 2)) for b in blk]
    masks.append(row4 == col4)
    heads = range(D_HEADS)
    chunks = range(nchunk)
    rs = [slice(j * c, (j + 1) * c) for j in chunks]
    ksb = [y.astype(BF16) for y in ks]
    prods = [[_dot_nt(jnp.concatenate([jnp.where(lane_head == h, x[rs[j], :], 0.0) for h in heads],
                                      axis=0).astype(BF16), y[rs[j], :])
              for x, y in zip(qs, ksb)] for j in chunks]
    sc = []
    for j in chunks:
        scores = jnp.zeros((D_HEADS * c, c), F32)
        for mask, pr in zip(masks, prods[j]):
            scores = scores + jnp.where(mask, pr, 0.0)
        sc.append(scores.astype(BF16))

    qe = (q * jnp.exp(cum)).astype(BF16)
    kd = k * jnp.exp(last - cum)
    e_last = jnp.exp(last)
    vh = [[v[rs[j], h * D_DV:(h + 1) * D_DV].astype(BF16) for h in heads] for j in chunks]
    intra = [[_dot(sc[j][h * c:(h + 1) * c, :], vh[j][h]) for h in heads] for j in chunks]
    upd = [[_dot_tn(vh[j][h], jnp.where(lane_head == h, kd[rs[j], :], 0.0).astype(BF16)) for h in heads]
           for j in chunks]
    return [(intra[j], upd[j], qe[rs[j], :], e_last[j * c:j * c + 1, :], og[rs[j], :]) for j in chunks]


def _gla_kernel(p_ref, w2_ref, gb_ref, ng_ref, o_ref, s_ref, *, nchunk):
    @pl.when(pl.program_id(0) == 0)
    def _():
        s_ref[...] = jnp.zeros_like(s_ref)

    c = D_CHUNK
    heads = range(D_HEADS)
    terms = _gla_chunk_terms(p_ref[...], w2_ref[...], gb_ref[...], nchunk)
    st = [s_ref[h] for h in heads]
    for j, (intra, upd, qe, e_last, og) in enumerate(terms):
        rs = slice(j * c, (j + 1) * c)
        o = [intra[h] + _dot_nt(qe, st[h].astype(BF16)) for h in heads]
        st = [st[h] * e_last + upd[h] for h in heads]
        for h in heads:
            vs = slice(h * D_DV, (h + 1) * D_DV)
            on = o[h] * lax.rsqrt(jnp.mean(o[h] * o[h], axis=-1, keepdims=True) + EPS) * ng_ref[...]
            ogh = og[:, vs]
            o_ref[rs, vs] = (on * (ogh * _sigmoid(ogh))).astype(o_ref.dtype)
    for h in heads:
        s_ref[h] = st[h]


def _gla(pd, w2, gate_b, norm_g):
    s = pd.shape[0]
    nck = min(D_CHUNKS_PER_STEP, s // D_CHUNK)
    rows = nck * D_CHUNK
    return pl.pallas_call(
        functools.partial(_gla_kernel, nchunk=nck),
        name="gla",
        grid=(s // rows,),
        in_specs=[pl.BlockSpec((rows, D_W), lambda i: (i, 0)),
                  pl.BlockSpec((LANES, D_QK), lambda i: (0, 0)),
                  pl.BlockSpec((1, D_QK), lambda i: (0, 0)),
                  pl.BlockSpec((1, D_DV), lambda i: (0, 0))],
        out_specs=pl.BlockSpec((rows, GROUP), lambda i: (i, 0)),
        out_shape=jax.ShapeDtypeStruct((s, GROUP), BF16),
        scratch_shapes=[pltpu.VMEM((D_HEADS, D_DV, D_QK), F32)],
        compiler_params=_params("arbitrary"),
    )(pd, w2, gate_b.reshape(1, D_QK), norm_g.reshape(1, D_DV))


FFN_ROWS = 512


def _ffn_up_kernel(h_ref, wg_ref, wu_ref, cwg_ref, cwu_ref, cbg_ref, cbu_ref, o_ref, wgb, wub, gbuf, ubuf):
    tm = h_ref.shape[0]
    rows = min(tm, FFN_ROWS)

    @pl.when(pl.program_id(1) == 0)
    def _():
        wgb[...] = wg_ref[...].astype(BF16)
        wub[...] = wu_ref[...].astype(BF16)
        gbuf[0:8, :] = jnp.zeros((8, gbuf.shape[1]), F32)
        ubuf[0:8, :] = jnp.zeros((8, ubuf.shape[1]), F32)

    def conv(buf, w, cw_ref, cb_ref, r0):
        buf[8 + r0:8 + r0 + rows, :] = _dot(h_ref[r0:r0 + rows, :], w[...])
        return (cb_ref[...] + buf[6 + r0:6 + r0 + rows, :] * cw_ref[0:1, :]
                + buf[7 + r0:7 + r0 + rows, :] * cw_ref[1:2, :]
                + buf[8 + r0:8 + r0 + rows, :] * cw_ref[2:3, :])

    for r0 in range(0, tm, rows):
        gate = conv(gbuf, wgb, cwg_ref, cbg_ref, r0)
        up = conv(ubuf, wub, cwu_ref, cbu_ref, r0)
        o_ref[r0:r0 + rows, :] = (gate * _sigmoid(gate) * up).astype(o_ref.dtype)
    gbuf[0:8, :] = gbuf[tm:tm + 8, :]
    ubuf[0:8, :] = ubuf[tm:tm + 8, :]


def _ffn_up(h, w_up_all, layer, conv_w, conv_b, tm=1024, tn=512):
    m, k = h.shape
    tm = min(tm, m)
    nj = D_FF // tn
    cb = conv_b.reshape(1, 2 * D_FF)
    return pl.pallas_call(
        _ffn_up_kernel,
        name="ffn_up",
        grid=(nj, m // tm),
        in_specs=[pl.BlockSpec((tm, k), lambda j, i: (i, 0)),
                  pl.BlockSpec((None, k, tn), lambda j, i: (layer, 0, j)),
                  pl.BlockSpec((None, k, tn), lambda j, i: (layer, 0, j + nj)),
                  pl.BlockSpec((3, tn), lambda j, i: (0, j)),
                  pl.BlockSpec((3, tn), lambda j, i: (0, j + nj)),
                  pl.BlockSpec((1, tn), lambda j, i: (0, j)),
                  pl.BlockSpec((1, tn), lambda j, i: (0, j + nj))],
        out_specs=pl.BlockSpec((tm, tn), lambda j, i: (i, j)),
        out_shape=jax.ShapeDtypeStruct((m, D_FF), BF16),
        scratch_shapes=[pltpu.VMEM((k, tn), BF16), pltpu.VMEM((k, tn), BF16),
                        pltpu.VMEM((tm + 8, tn), F32), pltpu.VMEM((tm + 8, tn), F32)],
        compiler_params=_params("parallel", "arbitrary"),
    )(h, w_up_all, w_up_all, conv_w, conv_w, cb, cb)


def _pad_cols(w, width):
    return jnp.pad(w, ((0, 0), (0, width - w.shape[1])))


def _pad_rows(w, height):
    return jnp.pad(w, ((0, height - w.shape[0]), (0, 0)))


def _split_w_in(w):
    a_in = 2 * GROUP
    b0 = a_in
    wa = w[:, :a_in]
    rkv = w[:, b0:b0 + 3 * GROUP]
    xw = w[:, b0 + 3 * GROUP:b0 + 3 * GROUP + 32]
    xa = w[:, b0 + 3 * GROUP + 32:b0 + 3 * GROUP + 64]
    xg = w[:, b0 + 3 * GROUP + 64:b0 + 3 * GROUP + 160]
    wb = jnp.concatenate([rkv, _pad_cols(xw, LANES), _pad_cols(xa, LANES), _pad_cols(xg, LANES)], axis=1)
    c0 = b0 + 3 * GROUP + 160
    wc = w[:, c0:c0 + 3 * GROUP]
    d0 = c0 + 3 * GROUP
    qkv = w[:, d0:d0 + 2 * D_QK + GROUP]
    dxg = w[:, d0 + 2 * D_QK + GROUP:d0 + 2 * D_QK + GROUP + 16]
    og = w[:, d0 + 2 * D_QK + GROUP + 16:d0 + 2 * D_QK + 2 * GROUP + 16]
    wd = jnp.concatenate([qkv, og, _pad_cols(dxg, LANES)], axis=1)
    return [t.astype(BF16) for t in (wa, wb, wc, wd)]


def _split_mu(mu):
    rkv = mu[:3 * GROUP]
    pads = [jnp.pad(mu[3 * GROUP + lo:3 * GROUP + hi], (0, LANES - (hi - lo))) for lo, hi in ((0, 32), (32, 64), (64, 160))]
    return jnp.concatenate([rkv] + pads).reshape(1, B_W)


def kernel(x, mix_norm_g, w_in, a_ln_g, a_ln_b, a_ws, a_bs, b_mu, b_w0, b_w2, b_a0, b_a2, b_g2, b_kk, b_ka, b_rk, b_lnx_g, b_lnx_b, d_gate_w2, d_gate_b, d_norm_g, w_out, ffn_norm_g, w_up, conv_w, conv_b, w_down, final_norm_g):
    bsz, s, d = x.shape
    depth = w_in.shape[0]
    cos, sin = _rope_tables(s)
    outs = []
    for bi in range(bsz):
        xc = x[bi]
        for l in range(depth):
            h = _rmsnorm(xc, mix_norm_g[l], BF16)
            wa, wb, wc, wd = _split_w_in(w_in[l])
            pa = _matmul(h, wa, F32)
            pb = _matmul(h, wb, F32)
            pc = _matmul(h, wc, F32)
            pd = _matmul(h, wd, F32)
            ya = _gmlp(pa, a_ln_g[l], a_ln_b[l], a_ws[l], a_bs[l])
            yb = _rwkv(pb, _split_mu(b_mu[l]), b_w0[l], _pad_rows(b_w2[l], LANES).astype(BF16), b_a0[l],
                       _pad_rows(b_a2[l], LANES).astype(BF16), _pad_rows(b_g2[l], LANES).astype(BF16),
                       b_kk[l], b_ka[l], b_rk[l], b_lnx_g[l], b_lnx_b[l])
            yct = _moba(pc, cos, sin)
            yd = _gla(pd, _pad_rows(d_gate_w2[l], LANES).astype(BF16), d_gate_b[l], d_norm_g[l])
            xc, h2 = _outproj(ya, yb, yct, yd, w_out[l].astype(BF16), xc, ffn_norm_g[l])
            act = _ffn_up(h2, w_up, l, conv_w[l], conv_b[l])
            xc = _matmul_res(act, w_down, l, xc)
        outs.append(_rmsnorm(xc, final_norm_g, F32))
    return jnp.stack(outs, axis=0)
```

```python
import functools
import math

import jax
import jax.numpy as jnp
import numpy as np
from jax import lax
from jax.experimental import pallas as pl
from jax.experimental.pallas import tpu as pltpu

F32 = jnp.float32
BF16 = jnp.bfloat16

EPS = 1e-6
GROUP = 512
LANES = 128
VMEM_LIMIT = 56 * 1024 * 1024

A_HEADS, A_CH, A_CHUNK = 4, 128, 128
B_HEADS, B_HD, B_CHUNK = 8, 64, 64
B_LNX_EPS = 64e-5
C_HEADS, C_HD, C_BLOCK, C_TOPK = 8, 64, 256, 3
ROPE_THETA = 10000.0
D_HEADS, D_DK, D_DV, D_CHUNK = 4, 64, 128, 64
D_GATE_TEMP = 16.0
D_FF = 5632
NEG = -1e30
LOG2E = 1.4426950408889634


def _params(*sem):
    return pltpu.CompilerParams(dimension_semantics=sem, vmem_limit_bytes=VMEM_LIMIT)


def _dot(a, b):
    return jnp.dot(a, b, preferred_element_type=F32)


def _dot_nt(a, b):
    return lax.dot_general(a, b, (((1,), (1,)), ((), ())), preferred_element_type=F32)


def _dot_tn(a, b):
    return lax.dot_general(a, b, (((0,), (0,)), ((), ())), preferred_element_type=F32)


def _split3(x):
    hi = x.astype(BF16)
    r1 = x - hi.astype(F32)
    mid = r1.astype(BF16)
    lo = (r1 - mid.astype(F32)).astype(BF16)
    return hi, mid, lo


def _split2(x):
    hi = x.astype(BF16)
    return hi, (x - hi.astype(F32)).astype(BF16)


def _dot_split(sel, x):
    hi, lo = _split2(x)
    return _dot(sel, hi) + _dot(sel, lo)


def _dot3_nt(a, b):
    ah, am, al = _split3(a)
    bh, bm, bl = _split3(b)
    return (_dot_nt(ah, bh) + (_dot_nt(ah, bm) + _dot_nt(am, bh))
            + (_dot_nt(am, bm) + _dot_nt(ah, bl) + _dot_nt(al, bh)))


def _sigmoid(x):
    return 1.0 / (1.0 + jnp.exp(-x))


def _softplus(x):
    return jnp.maximum(x, 0.0) + jnp.log(1.0 + jnp.exp(-jnp.abs(x)))


def _iota2(shape, axis):
    return lax.broadcasted_iota(jnp.int32, shape, axis)


def _rmsnorm_kernel(x_ref, g_ref, o_ref):
    x = x_ref[...]
    ms = jnp.mean(x * x, axis=-1, keepdims=True)
    o_ref[...] = (x * lax.rsqrt(ms + EPS) * g_ref[...]).astype(o_ref.dtype)


def _rmsnorm(x, g, out_dtype, tm=512):
    m, d = x.shape
    return pl.pallas_call(
        _rmsnorm_kernel,
        name="rmsnorm",
        grid=(m // tm,),
        in_specs=[pl.BlockSpec((tm, d), lambda i: (i, 0)),
                  pl.BlockSpec((1, d), lambda i: (0, 0))],
        out_specs=pl.BlockSpec((tm, d), lambda i: (i, 0)),
        out_shape=jax.ShapeDtypeStruct((m, d), out_dtype),
        compiler_params=_params("parallel"),
    )(x, g.reshape(1, d))


def _mm_kernel(x_ref, w_ref, o_ref):
    o_ref[...] = _dot(x_ref[...], w_ref[...]).astype(o_ref.dtype)


def _matmul(x, w, out_dtype, tm=512):
    m, k = x.shape
    n = w.shape[1]
    return pl.pallas_call(
        _mm_kernel,
        name="inproj",
        grid=(m // tm,),
        in_specs=[pl.BlockSpec((tm, k), lambda i: (i, 0)),
                  pl.BlockSpec((k, n), lambda i: (0, 0))],
        out_specs=pl.BlockSpec((tm, n), lambda i: (i, 0)),
        out_shape=jax.ShapeDtypeStruct((m, n), out_dtype),
        compiler_params=_params("parallel"),
    )(x, w)


def _mm_res_kernel(x_ref, w_ref, r_ref, o_ref, wb):
    @pl.when(pl.program_id(1) == 0)
    def _():
        wb[...] = w_ref[...].astype(BF16)

    o_ref[...] = r_ref[...] + _dot(x_ref[...], wb[...])


def _matmul_res(x, w_all, layer, res, tm=512, tn=512):
    m, k = x.shape
    n = w_all.shape[2]
    tm = min(tm, m)
    return pl.pallas_call(
        _mm_res_kernel,
        name="ffn_down",
        grid=(n // tn, m // tm),
        in_specs=[pl.BlockSpec((tm, k), lambda j, i: (i, 0)),
                  pl.BlockSpec((None, k, tn), lambda j, i: (layer, 0, j)),
                  pl.BlockSpec((tm, tn), lambda j, i: (i, j))],
        out_specs=pl.BlockSpec((tm, tn), lambda j, i: (i, j)),
        out_shape=jax.ShapeDtypeStruct((m, n), F32),
        scratch_shapes=[pltpu.VMEM((k, tn), BF16)],
        compiler_params=_params("parallel", "arbitrary"),
    )(x, w_all, res)


def _outproj_kernel(ya_ref, yb_ref, yct_ref, yd_ref, w_ref, x_ref, g_ref, o_ref, h_ref):
    yc = yct_ref[...].T.astype(BF16)
    acc = _dot(ya_ref[...], w_ref[0:GROUP, :])
    acc += _dot(yb_ref[...], w_ref[GROUP:2 * GROUP, :])
    acc += _dot(yc, w_ref[2 * GROUP:3 * GROUP, :])
    acc += _dot(yd_ref[...], w_ref[3 * GROUP:4 * GROUP, :])
    x = x_ref[...] + acc
    o_ref[...] = x
    ms = jnp.mean(x * x, axis=-1, keepdims=True)
    h_ref[...] = (x * lax.rsqrt(ms + EPS) * g_ref[...]).astype(h_ref.dtype)


def _outproj(ya, yb, yct, yd, w, x, g, tm=256):
    m, n = x.shape
    row = pl.BlockSpec((tm, GROUP), lambda i: (i, 0))
    full = pl.BlockSpec((tm, n), lambda i: (i, 0))
    return pl.pallas_call(
        _outproj_kernel,
        name="outproj",
        grid=(m // tm,),
        in_specs=[row, row,
                  pl.BlockSpec((GROUP, tm), lambda i: (0, i)),
                  row,
                  pl.BlockSpec((4 * GROUP, n), lambda i: (0, 0)),
                  full,
                  pl.BlockSpec((1, n), lambda i: (0, 0))],
        out_specs=[full, full],
        out_shape=[jax.ShapeDtypeStruct((m, n), F32), jax.ShapeDtypeStruct((m, n), BF16)],
        compiler_params=_params("parallel"),
    )(ya, yb, yct, yd, w, x, g.reshape(1, n))


def _gmlp_kernel(p_ref, lng_ref, lnb_ref, ws_ref, bsb_ref, o_ref, *, nchunk):
    p = p_ref[...]
    z = 0.5 * p * (1.0 + jnp.tanh(math.sqrt(2.0 / math.pi) * (p + 0.044715 * (p * p * p))))
    u = z[:, :GROUP]
    v = z[:, GROUP:]
    mu = jnp.mean(v, axis=-1, keepdims=True)
    vc = v - mu
    var = jnp.mean(vc * vc, axis=-1, keepdims=True)
    vn = (vc * lax.rsqrt(var + EPS) * lng_ref[...] + lnb_ref[...]).astype(BF16)
    tri = _iota2((A_CHUNK, A_CHUNK), 1) <= _iota2((A_CHUNK, A_CHUNK), 0)
    for h in range(A_HEADS):
        w = jnp.where(tri, ws_ref[h], 0.0).astype(BF16)
        cs = slice(h * A_CH, (h + 1) * A_CH)
        for c in range(nchunk):
            rs = slice(c * A_CHUNK, (c + 1) * A_CHUNK)
            mixed = _dot(w, vn[rs, cs]) + bsb_ref[:, cs]
            o_ref[rs, cs] = (u[rs, cs] * mixed).astype(o_ref.dtype)


def _gmlp(pa, ln_g, ln_b, ws, bs, nchunk=4):
    s = pa.shape[0]
    tm = nchunk * A_CHUNK
    bsb = jnp.repeat(bs.T, A_CH, axis=1)
    vec = pl.BlockSpec((1, GROUP), lambda i: (0, 0))
    return pl.pallas_call(
        functools.partial(_gmlp_kernel, nchunk=nchunk),
        name="gmlp",
        grid=(s // tm,),
        in_specs=[pl.BlockSpec((tm, 2 * GROUP), lambda i: (i, 0)), vec, vec,
                  pl.BlockSpec((A_HEADS, A_CHUNK, A_CHUNK), lambda i: (0, 0, 0)),
                  pl.BlockSpec((A_CHUNK, GROUP), lambda i: (0, 0))],
        out_specs=pl.BlockSpec((tm, GROUP), lambda i: (i, 0)),
        out_shape=jax.ShapeDtypeStruct((s, GROUP), BF16),
        compiler_params=_params("parallel"),
    )(pa, ln_g.reshape(1, GROUP), ln_b.reshape(1, GROUP), ws, bsb)


B_W = 3 * GROUP + 3 * LANES
B_TM = 512
B_STATE_CHUNKS = 4
B_CHUNKS_PER_STEP = 4


def _rwkv_prep_kernel(p_ref, pprev_ref, mu_ref, w0_ref, w2_ref, a0_ref, a2_ref, g2_ref,
                      kkw_ref, ka_ref, rk_ref,
                      at_ref, bt_ref, kt_ref, rt_ref, v_ref, bd_ref, kd_ref, pc_ref, g_ref, bonus_ref):
    i = pl.program_id(0)
    p = p_ref[...]
    tm = p.shape[0]
    last = jnp.where(i == 0, 0.0, pprev_ref[7:8, :])
    prev = jnp.where(_iota2(p.shape, 0) == 0, last, pltpu.roll(p, 1, axis=0))
    xs = p + (prev - p) * mu_ref[...]
    r = xs[:, 0:GROUP]
    k = xs[:, GROUP:2 * GROUP]
    v = xs[:, 2 * GROUP:3 * GROUP]
    xw = xs[:, 3 * GROUP:3 * GROUP + LANES]
    xa = xs[:, 3 * GROUP + LANES:3 * GROUP + 2 * LANES]
    xg = xs[:, 3 * GROUP + 2 * LANES:3 * GROUP + 3 * LANES]
    w = -_softplus(-(w0_ref[...] + _dot(jnp.tanh(xw).astype(BF16), w2_ref[...]))) - 0.5
    logd = -jnp.exp(w)
    a = _sigmoid(a0_ref[...] + _dot(xa.astype(BF16), a2_ref[...]))
    g_ref[...] = _dot(_sigmoid(xg).astype(BF16), g2_ref[...])

    lane_head = _iota2((GROUP, GROUP), 0) // B_HD == _iota2((GROUP, GROUP), 1) // B_HD
    seg = lane_head.astype(BF16)
    kk = k * kkw_ref[...]
    nrm = jnp.sqrt(_dot_split_rhs(kk * kk, seg))
    kk = kk / jnp.maximum(nrm, 1e-12)
    k2 = k * (1.0 + (a - 1.0) * ka_ref[...])
    b = kk * a
    bonus_ref[...] = _dot_split_rhs(r * k2 * rk_ref[...], seg) * v

    row = _iota2((tm, tm), 0)
    col = _iota2((tm, tm), 1)
    same = row // B_CHUNK == col // B_CHUNK
    cl = _dot_split(jnp.logical_and(same, col <= row).astype(BF16), logd)
    cl_last = _dot_split(same.astype(BF16), logd)
    e_neg = jnp.exp(-cl)
    e_end = jnp.exp(cl_last - cl)
    at = -kk * jnp.exp(cl - logd)
    bt = b * e_neg
    kt = k2 * e_neg
    rt = r * jnp.exp(cl)
    bd = b * e_end
    kd = k2 * e_end
    for h in range(B_HEADS):
        cs = slice(h * B_HD, (h + 1) * B_HD)
        at_ref[h] = at[:, cs].astype(BF16)
        bt_ref[h] = bt[:, cs].astype(BF16)
        kt_ref[h] = kt[:, cs].astype(BF16)
        rt_ref[h] = rt[:, cs].astype(BF16)
        v_ref[h] = v[:, cs].astype(BF16)
        bd_ref[h] = bd[:, cs].astype(BF16)
        kd_ref[h] = kd[:, cs].astype(BF16)
    pc = jnp.exp(cl_last)
    for c in range(tm // B_CHUNK):
        pc_ref[c] = pc[c * B_CHUNK:c * B_CHUNK + 1, :]


def _dot_split_rhs(x, sel):
    hi, lo = _split2(x)
    return _dot(hi, sel) + _dot(lo, sel)


def _rwkv_chunk_kernel(at_ref, bt_ref, kt_ref, rt_ref, v_ref, bd_ref, kd_ref, r2_ref, y0_ref, m2_ref, sa_ref):
    c = B_CHUNK
    row = _iota2((2 * c, 2 * c), 0)
    col = _iota2((2 * c, 2 * c), 1) % c
    keep = jnp.logical_or(col < row % c, jnp.logical_and(row >= c, col == row % c))
    zeros = jnp.zeros((c, B_HD), BF16)
    nck = m2_ref.shape[0]
    units = [(j, h) for j in range(nck) for h in range(B_HEADS)]
    idx = range(len(units))

    def ld(ref, u):
        j, h = units[u]
        return ref[h, j * c:(j + 1) * c, :]

    big = [jnp.where(keep, _dot_nt(jnp.concatenate([ld(at_ref, u), ld(rt_ref, u)], axis=0),
                                   jnp.concatenate([ld(bt_ref, u), ld(kt_ref, u)], axis=0)), 0.0)
           for u in idx]
    n = [big[u][:c, :c] for u in idx]
    akv = [_dot(big[u][:c, c:].astype(BF16), ld(v_ref, u)) for u in idx]
    x = [jnp.concatenate([ld(at_ref, u).astype(F32), akv[u]], axis=1) for u in idx]
    for lvl in range(6):
        nb = [n[u].astype(BF16) for u in idx]
        x = [x[u] + _dot(nb[u], x[u].astype(BF16)) for u in idx]
        if lvl < 5:
            n = [_dot(nb[u], nb[u]) for u in idx]
    xb = [x[u].astype(BF16) for u in idx]
    ry = [_dot(big[u][c:, :].astype(BF16),
               jnp.concatenate([xb[u], jnp.concatenate([zeros, ld(v_ref, u)], axis=1)], axis=0))
          for u in idx]
    ms = [_dot_tn(xb[u], ld(bd_ref, u)) for u in idx]
    vk = [_dot_tn(ld(v_ref, u), ld(kd_ref, u)) for u in idx]
    for u in idx:
        j, h = units[u]
        rs = slice(j * c, (j + 1) * c)
        r2_ref[h, rs, :] = (ld(rt_ref, u).astype(F32) + ry[u][:, :B_HD]).astype(BF16)
        y0_ref[h, rs, :] = ry[u][:, B_HD:]
        m2_ref[j, h] = ms[u][:B_HD, :].astype(BF16)
        sa_ref[j, h] = ms[u][B_HD:, :] + vk[u]


def _rwkv_state_kernel(r2_ref, y0_ref, m2_ref, sa_ref, pc_ref, g_ref, bonus_ref, lng_ref, lnb_ref,
                       o_ref, s_ref, y_ref, *, nchunk):
    @pl.when(pl.program_id(0) == 0)
    def _():
        s_ref[...] = jnp.zeros_like(s_ref)

    c = B_CHUNK
    for j in range(nchunk):
        rs = slice(j * c, (j + 1) * c)
        pc = pc_ref[j]
        heads = range(B_HEADS)
        st = [s_ref[h] for h in heads]
        sb = [st[h].astype(BF16) for h in heads]
        sm = [_dot(sb[h], m2_ref[j, h]) for h in heads]
        ys = [_dot_nt(r2_ref[h, rs, :], sb[h]) for h in heads]
        for h in heads:
            cs = slice(h * B_HD, (h + 1) * B_HD)
            s_ref[h] = st[h] * pc[:, cs] + sm[h] + sa_ref[j, h]
            y_ref[rs, cs] = ys[h] + y0_ref[h, rs, :]
    seg = (_iota2((GROUP, GROUP), 0) // B_HD == _iota2((GROUP, GROUP), 1) // B_HD).astype(BF16)
    y = y_ref[...]
    yc = y - _dot_split_rhs(y, seg) * (1.0 / B_HD)
    yv = _dot_split_rhs(yc * yc, seg) * (1.0 / B_HD)
    y = yc * lax.rsqrt(yv + B_LNX_EPS) * lng_ref[...] + lnb_ref[...] + bonus_ref[...]
    o_ref[...] = (y * g_ref[...]).astype(o_ref.dtype)


def _rwkv(pb, mu, w0, w2, a0, a2, g2, k_k, k_a, r_k, lnx_g, lnx_b):
    s = pb.shape[0]
    tm = min(B_TM, s)
    nch = s // B_CHUNK
    vecw = pl.BlockSpec((1, B_W), lambda i: (0, 0))
    vec = pl.BlockSpec((1, GROUP), lambda i: (0, 0))
    lora = pl.BlockSpec((LANES, GROUP), lambda i: (0, 0))
    hm = pl.BlockSpec((B_HEADS, tm, B_HD), lambda i: (0, i, 0))
    full = pl.BlockSpec((tm, GROUP), lambda i: (i, 0))
    hm_shape = jax.ShapeDtypeStruct((B_HEADS, s, B_HD), BF16)
    outs = pl.pallas_call(
        _rwkv_prep_kernel,
        name="rwkv_prep",
        grid=(s // tm,),
        in_specs=[pl.BlockSpec((tm, B_W), lambda i: (i, 0)),
                  pl.BlockSpec((8, B_W), lambda i: (jnp.maximum(i * (tm // 8) - 1, 0), 0)),
                  vecw, vec, lora, vec, lora, lora, vec, vec, vec],
        out_specs=[hm] * 7 + [pl.BlockSpec((tm // B_CHUNK, 1, GROUP), lambda i: (i, 0, 0)), full, full],
        out_shape=[hm_shape] * 7 + [jax.ShapeDtypeStruct((nch, 1, GROUP), F32),
                                    jax.ShapeDtypeStruct((s, GROUP), F32),
                                    jax.ShapeDtypeStruct((s, GROUP), F32)],
        compiler_params=_params("parallel"),
    )(pb, pb, mu, w0.reshape(1, GROUP), w2, a0.reshape(1, GROUP), a2, g2,
      k_k.reshape(1, GROUP), k_a.reshape(1, GROUP), r_k.reshape(1, GROUP))
    at, bt, kt, rt, v, bd, kd, pc, g, bonus = outs

    c = B_CHUNK
    ncs = min(B_CHUNKS_PER_STEP, nch)
    hmc = pl.BlockSpec((B_HEADS, ncs * c, B_HD), lambda i: (0, i, 0))
    sq = pl.BlockSpec((ncs, B_HEADS, B_HD, B_HD), lambda i: (i, 0, 0, 0))
    r2, y0, m2, sa = pl.pallas_call(
        _rwkv_chunk_kernel,
        name="rwkv_chunk",
        grid=(nch // ncs,),
        in_specs=[hmc] * 7,
        out_specs=[hmc, hmc, sq, sq],
        out_shape=[hm_shape, jax.ShapeDtypeStruct((B_HEADS, s, B_HD), F32),
                   jax.ShapeDtypeStruct((nch, B_HEADS, B_HD, B_HD), BF16),
                   jax.ShapeDtypeStruct((nch, B_HEADS, B_HD, B_HD), F32)],
        compiler_params=_params("parallel"),
    )(at, bt, kt, rt, v, bd, kd)

    nck = min(B_STATE_CHUNKS, nch)
    rows = nck * c
    hmr = pl.BlockSpec((B_HEADS, rows, B_HD), lambda i: (0, i, 0))
    sqr = pl.BlockSpec((nck, B_HEADS, B_HD, B_HD), lambda i: (i, 0, 0, 0))
    fullr = pl.BlockSpec((rows, GROUP), lambda i: (i, 0))
    return pl.pallas_call(
        functools.partial(_rwkv_state_kernel, nchunk=nck),
        name="rwkv_state",
        grid=(nch // nck,),
        in_specs=[hmr, hmr, sqr, sqr, pl.BlockSpec((nck, 1, GROUP), lambda i: (i, 0, 0)), fullr, fullr, vec, vec],
        out_specs=fullr,
        out_shape=jax.ShapeDtypeStruct((s, GROUP), BF16),
        scratch_shapes=[pltpu.VMEM((B_HEADS, B_HD, B_HD), F32), pltpu.VMEM((rows, GROUP), F32)],
        compiler_params=_params("arbitrary"),
    )(r2, y0, m2, sa, pc, g, bonus, lnx_g.reshape(1, GROUP), lnx_b.reshape(1, GROUP))


def _moba_prep_kernel(p_ref, cos_ref, sin_ref, q_ref, k_ref, vt_ref, bias_ref, kmean_ref):
    i = pl.program_id(0)
    nb = kmean_ref.shape[0]

    @pl.when(i == 0)
    def _():
        kmean_ref[...] = jnp.zeros_like(kmean_ref)

    p = p_ref[...]
    cos = cos_ref[...]
    sin = sin_ref[...]
    first_half = (_iota2((C_BLOCK, GROUP), 1) % C_HD) < C_HD // 2

    def rope(x):
        rot = jnp.where(first_half, pltpu.roll(x, GROUP - C_HD // 2, axis=1), pltpu.roll(x, C_HD // 2, axis=1))
        return x * cos + rot * sin

    q = rope(p[:, 0:GROUP]) * (C_HD ** -0.5)
    k = rope(p[:, GROUP:2 * GROUP])
    v = p[:, 2 * GROUP:3 * GROUP]
    vt = v.T
    for h in range(C_HEADS):
        cs = slice(h * C_HD, (h + 1) * C_HD)
        q_ref[h] = (q[:, cs] * LOG2E).astype(BF16)
        k_ref[h, 0] = k[:, cs].astype(BF16)
        vt_ref[h, 0] = vt[cs, :].astype(BF16)

    km = kmean_ref[...]
    lane_h = _iota2((nb, GROUP), 1) // C_HD
    km_heads = jnp.concatenate([jnp.where(lane_h == h, km, 0.0) for h in range(C_HEADS)], axis=0)
    sc = _dot3_nt(km_heads, q).reshape(C_HEADS, nb, C_BLOCK)
    blk_i = _iota2((C_HEADS, nb, C_BLOCK), 1)
    blk = blk_i.astype(F32)
    cur = jnp.where(blk_i < i, sc, -jnp.inf)
    bias = jnp.full((C_HEADS, nb, C_BLOCK), NEG, F32)
    for _ in range(C_TOPK):
        m = jnp.max(cur, axis=1, keepdims=True)
        first = jnp.min(jnp.where(cur == m, blk, float(nb)), axis=1, keepdims=True)
        pick = jnp.logical_and(blk == first, m > -jnp.inf)
        bias = jnp.where(pick, 0.0, bias)
        cur = jnp.where(pick, -jnp.inf, cur)
    bias_ref[...] = bias
    kmean_ref[pl.ds(i, 1), :] = jnp.mean(k, axis=0, keepdims=True)


C_UNROLL = 8


def _moba_attn_kernel(q_ref, k_ref, vt_ref, bias_ref, o_ref, s_buf):
    i = pl.program_id(1)
    nb = k_ref.shape[1]
    q = q_ref[0]
    blk_a = 2 * i
    blk_b = 2 * i + 1
    slot_a = nb + C_UNROLL - 2
    slot_b = nb + C_UNROLL - 1
    shape = (C_BLOCK, 2 * C_BLOCK)
    key = _iota2(shape, 0)
    qry = _iota2(shape, 1)
    bias_a = jnp.where(qry < C_BLOCK, jnp.where(key <= qry, 0.0, NEG), bias_ref[0, pl.ds(blk_a, 1), :])
    bias_b = jnp.where(jnp.logical_and(qry >= C_BLOCK, key <= qry - C_BLOCK), 0.0, NEG)
    st_a = _dot_nt(k_ref[0, blk_a], q) + bias_a
    st_b = _dot_nt(k_ref[0, blk_b], q) + bias_b
    s_buf[slot_a] = st_a
    s_buf[slot_b] = st_b
    m = jnp.maximum(jnp.max(st_a, axis=0, keepdims=True), jnp.max(st_b, axis=0, keepdims=True))
    npast = blk_a
    ngroups = (npast + C_UNROLL - 1) // C_UNROLL

    def scores(t, m):
        js = [t * C_UNROLL + u for u in range(C_UNROLL)]
        jcs = [jnp.minimum(j, nb - 1) for j in js]
        sts = [_dot_nt(k_ref[0, jc], q) for jc in jcs]
        for j, jc, st in zip(js, jcs, sts):
            st = st + jnp.where(j < npast, bias_ref[0, pl.ds(jc, 1), :], NEG)
            s_buf[j] = st
            m = jnp.maximum(m, jnp.max(st, axis=0, keepdims=True))
        return m

    m = lax.fori_loop(0, ngroups, scores, m)

    p_a = jnp.exp2(s_buf[slot_a] - m)
    p_b = jnp.exp2(s_buf[slot_b] - m)
    l = jnp.sum(p_a, axis=0, keepdims=True) + jnp.sum(p_b, axis=0, keepdims=True)
    acc = _dot(vt_ref[0, blk_a], p_a.astype(BF16)) + _dot(vt_ref[0, blk_b], p_b.astype(BF16))

    def values(t, carry):
        l, acc = carry
        js = [t * C_UNROLL + u for u in range(C_UNROLL)]
        ps = [jnp.exp2(s_buf[j] - m) for j in js]
        pv = [_dot(vt_ref[0, jnp.minimum(j, nb - 1)], p.astype(BF16)) for j, p in zip(js, ps)]
        for p, o in zip(ps, pv):
            l = l + jnp.sum(p, axis=0, keepdims=True)
            acc = acc + o
        return l, acc

    l, acc = lax.fori_loop(0, ngroups, values, (l, acc))
    o_ref[0] = acc / l


def _moba(pc, cos, sin):
    s = pc.shape[0]
    nb = s // C_BLOCK
    tile = pl.BlockSpec((C_BLOCK, GROUP), lambda i: (i, 0))
    q, k, vt, bias = pl.pallas_call(
        _moba_prep_kernel,
        name="moba_prep",
        grid=(nb,),
        in_specs=[pl.BlockSpec((C_BLOCK, 3 * GROUP), lambda i: (i, 0)), tile, tile],
        out_specs=[pl.BlockSpec((C_HEADS, C_BLOCK, C_HD), lambda i: (0, i, 0)),
                   pl.BlockSpec((C_HEADS, 1, C_BLOCK, C_HD), lambda i: (0, i, 0, 0)),
                   pl.BlockSpec((C_HEADS, 1, C_HD, C_BLOCK), lambda i: (0, i, 0, 0)),
                   pl.BlockSpec((C_HEADS, nb, C_BLOCK), lambda i: (0, 0, i))],
        out_shape=[jax.ShapeDtypeStruct((C_HEADS, s, C_HD), BF16),
                   jax.ShapeDtypeStruct((C_HEADS, nb, C_BLOCK, C_HD), BF16),
                   jax.ShapeDtypeStruct((C_HEADS, nb, C_HD, C_BLOCK), BF16),
                   jax.ShapeDtypeStruct((C_HEADS, nb, s), F32)],
        scratch_shapes=[pltpu.VMEM((nb, GROUP), F32)],
        compiler_params=_params("arbitrary"),
    )(pc, cos, sin)
    ot = pl.pallas_call(
        _moba_attn_kernel,
        grid=(C_HEADS, nb // 2),
        in_specs=[pl.BlockSpec((1, 2 * C_BLOCK, C_HD), lambda h, i: (h, i, 0)),
                  pl.BlockSpec((1, nb, C_BLOCK, C_HD), lambda h, i: (h, 0, 0, 0)),
                  pl.BlockSpec((1, nb, C_HD, C_BLOCK), lambda h, i: (h, 0, 0, 0)),
                  pl.BlockSpec((1, nb, 2 * C_BLOCK), lambda h, i: (h, 0, i))],
        out_specs=pl.BlockSpec((1, C_HD, 2 * C_BLOCK), lambda h, i: (h, 0, i)),
        out_shape=jax.ShapeDtypeStruct((C_HEADS, C_HD, s), F32),
        scratch_shapes=[pltpu.VMEM((nb + C_UNROLL, C_BLOCK, 2 * C_BLOCK), F32)],
        compiler_params=_params("parallel", "parallel"),
        name="moba_attn",
    )(q, k, vt, bias)
    return ot.reshape(GROUP, s)


def _rope_tables(s):
    half = C_HD // 2
    inv = ROPE_THETA ** (-jnp.arange(half, dtype=F32) / half)
    ang = jnp.arange(s, dtype=F32)[:, None] * inv[None, :]
    cos = jnp.cos(ang)
    sin = jnp.sin(ang)
    cos_h = jnp.concatenate([cos, cos], axis=1)
    sin_h = jnp.concatenate([-sin, sin], axis=1)
    return jnp.tile(cos_h, (1, C_HEADS)), jnp.tile(sin_h, (1, C_HEADS))


D_W = 2 * D_HEADS * D_DK + 2 * GROUP + LANES
D_QK = D_HEADS * D_DK
D_LEVELS = 6
D_CHUNKS_PER_STEP = 4


def _gla_chunk_terms(p, w2, gb, nchunk):
    c = D_CHUNK
    n = nchunk * c
    q = p[:, 0:D_QK] * (D_DK ** -0.5)
    k = p[:, D_QK:2 * D_QK]
    v = p[:, 2 * D_QK:2 * D_QK + GROUP]
    og = p[:, 2 * D_QK + GROUP:2 * D_QK + 2 * GROUP]
    xg = p[:, 2 * D_QK + 2 * GROUP:]
    gl = -_softplus(-(_dot(xg.astype(BF16), w2) + gb)) / D_GATE_TEMP
    row = _iota2((n, n), 0)
    col = _iota2((n, n), 1)
    same = row // c == col // c
    cum = _dot_split(jnp.logical_and(same, col <= row).astype(BF16), gl)
    last = _dot_split((col == (row // c) * c + (c - 1)).astype(BF16), cum)

    lane_head = _iota2((c, D_QK), 1) // D_DK
    row4 = _iota2((D_HEADS * c, c), 0) % c
    col4 = _iota2((D_HEADS * c, c), 1)
    levels = range(D_LEVELS)
    blk = [c >> lvl for lvl in levels]
    rho = [_dot_split((col == (row // b) * b + (b // 2 - 1)).astype(BF16), cum) for b in blk]
    qs = [q * jnp.exp(jnp.minimum(cum - r, 0.0)) for r in rho] + [q]
    ks = [k * jnp.exp(jnp.minimum(r - cum, 0.0)) for r in rho] + [k]
    masks = [jnp.logical_and(row4 // b == col4 // b,
                             jnp.logical_and(row4 % b >= b // 2, col4 % b < b // 2)) for b in blk]
    masks.append(row4 == col4)
    heads = range(D_HEADS)
    chunks = range(nchunk)
    rs = [slice(j * c, (j + 1) * c) for j in chunks]
    ksb = [y.astype(BF16) for y in ks]
    prods = [[_dot_nt(jnp.concatenate([jnp.where(lane_head == h, x[rs[j], :], 0.0) for h in heads],
                                      axis=0).astype(BF16), y[rs[j], :])
              for x, y in zip(qs, ksb)] for j in chunks]
    sc = []
    for j in chunks:
        scores = jnp.zeros((D_HEADS * c, c), F32)
        for mask, pr in zip(masks, prods[j]):
            scores = scores + jnp.where(mask, pr, 0.0)
        sc.append(scores.astype(BF16))

    qe = (q * jnp.exp(cum)).astype(BF16)
    kd = k * jnp.exp(last - cum)
    e_last = jnp.exp(last)
    vh = [[v[rs[j], h * D_DV:(h + 1) * D_DV].astype(BF16) for h in heads] for j in chunks]
    intra = [[_dot(sc[j][h * c:(h + 1) * c, :], vh[j][h]) for h in heads] for j in chunks]
    upd = [[_dot_tn(vh[j][h], jnp.where(lane_head == h, kd[rs[j], :], 0.0).astype(BF16)) for h in heads]
           for j in chunks]
    return [(intra[j], upd[j], qe[rs[j], :], e_last[j * c:j * c + 1, :], og[rs[j], :]) for j in chunks]


def _gla_kernel(p_ref, w2_ref, gb_ref, ng_ref, o_ref, s_ref, *, nchunk):
    @pl.when(pl.program_id(0) == 0)
    def _():
        s_ref[...] = jnp.zeros_like(s_ref)

    c = D_CHUNK
    heads = range(D_HEADS)
    terms = _gla_chunk_terms(p_ref[...], w2_ref[...], gb_ref[...], nchunk)
    st = [s_ref[h] for h in heads]
    for j, (intra, upd, qe, e_last, og) in enumerate(terms):
        rs = slice(j * c, (j + 1) * c)
        o = [intra[h] + _dot_nt(qe, st[h].astype(BF16)) for h in heads]
        st = [st[h] * e_last + upd[h] for h in heads]
        for h in heads:
            vs = slice(h * D_DV, (h + 1) * D_DV)
            on = o[h] * lax.rsqrt(jnp.mean(o[h] * o[h], axis=-1, keepdims=True) + EPS) * ng_ref[...]
            ogh = og[:, vs]
            o_ref[rs, vs] = (on * (ogh * _sigmoid(ogh))).astype(o_ref.dtype)
    for h in heads:
        s_ref[h] = st[h]


def _gla(pd, w2, gate_b, norm_g):
    s = pd.shape[0]
    nck = min(D_CHUNKS_PER_STEP, s // D_CHUNK)
    rows = nck * D_CHUNK
    return pl.pallas_call(
        functools.partial(_gla_kernel, nchunk=nck),
        name="gla",
        grid=(s // rows,),
        in_specs=[pl.BlockSpec((rows, D_W), lambda i: (i, 0)),
                  pl.BlockSpec((LANES, D_QK), lambda i: (0, 0)),
                  pl.BlockSpec((1, D_QK), lambda i: (0, 0)),
                  pl.BlockSpec((1, D_DV), lambda i: (0, 0))],
        out_specs=pl.BlockSpec((rows, GROUP), lambda i: (i, 0)),
        out_shape=jax.ShapeDtypeStruct((s, GROUP), BF16),
        scratch_shapes=[pltpu.VMEM((D_HEADS, D_DV, D_QK), F32)],
        compiler_params=_params("arbitrary"),
    )(pd, w2, gate_b.reshape(1, D_QK), norm_g.reshape(1, D_DV))


FFN_ROWS = 512


def _ffn_up_kernel(h_ref, wg_ref, wu_ref, cwg_ref, cwu_ref, cbg_ref, cbu_ref, o_ref, wgb, wub, gbuf, ubuf):
    tm = h_ref.shape[0]
    rows = min(tm, FFN_ROWS)

    @pl.when(pl.program_id(1) == 0)
    def _():
        wgb[...] = wg_ref[...].astype(BF16)
        wub[...] = wu_ref[...].astype(BF16)
        gbuf[0:8, :] = jnp.zeros((8, gbuf.shape[1]), F32)
        ubuf[0:8, :] = jnp.zeros((8, ubuf.shape[1]), F32)

    def conv(buf, w, cw_ref, cb_ref, r0):
        buf[8 + r0:8 + r0 + rows, :] = _dot(h_ref[r0:r0 + rows, :], w[...])
        return (cb_ref[...] + buf[6 + r0:6 + r0 + rows, :] * cw_ref[0:1, :]
                + buf[7 + r0:7 + r0 + rows, :] * cw_ref[1:2, :]
                + buf[8 + r0:8 + r0 + rows, :] * cw_ref[2:3, :])

    for r0 in range(0, tm, rows):
        gate = conv(gbuf, wgb, cwg_ref, cbg_ref, r0)
        up = conv(ubuf, wub, cwu_ref, cbu_ref, r0)
        o_ref[r0:r0 + rows, :] = (gate * _sigmoid(gate) * up).astype(o_ref.dtype)
    gbuf[0:8, :] = gbuf[tm:tm + 8, :]
    ubuf[0:8, :] = ubuf[tm:tm + 8, :]


def _ffn_up(h, w_up_all, layer, conv_w, conv_b, tm=1024, tn=512):
    m, k = h.shape
    tm = min(tm, m)
    nj = D_FF // tn
    cb = conv_b.reshape(1, 2 * D_FF)
    return pl.pallas_call(
        _ffn_up_kernel,
        name="ffn_up",
        grid=(nj, m // tm),
        in_specs=[pl.BlockSpec((tm, k), lambda j, i: (i, 0)),
                  pl.BlockSpec((None, k, tn), lambda j, i: (layer, 0, j)),
                  pl.BlockSpec((None, k, tn), lambda j, i: (layer, 0, j + nj)),
                  pl.BlockSpec((3, tn), lambda j, i: (0, j)),
                  pl.BlockSpec((3, tn), lambda j, i: (0, j + nj)),
                  pl.BlockSpec((1, tn), lambda j, i: (0, j)),
                  pl.BlockSpec((1, tn), lambda j, i: (0, j + nj))],
        out_specs=pl.BlockSpec((tm, tn), lambda j, i: (i, j)),
        out_shape=jax.ShapeDtypeStruct((m, D_FF), BF16),
        scratch_shapes=[pltpu.VMEM((k, tn), BF16), pltpu.VMEM((k, tn), BF16),
                        pltpu.VMEM((tm + 8, tn), F32), pltpu.VMEM((tm + 8, tn), F32)],
        compiler_params=_params("parallel", "arbitrary"),
    )(h, w_up_all, w_up_all, conv_w, conv_w, cb, cb)


def _pad_cols(w, width):
    return jnp.pad(w, ((0, 0), (0, width - w.shape[1])))


def _pad_rows(w, height):
    return jnp.pad(w, ((0, height - w.shape[0]), (0, 0)))


def _split_w_in(w):
    a_in = 2 * GROUP
    b0 = a_in
    wa = w[:, :a_in]
    rkv = w[:, b0:b0 + 3 * GROUP]
    xw = w[:, b0 + 3 * GROUP:b0 + 3 * GROUP + 32]
    xa = w[:, b0 + 3 * GROUP + 32:b0 + 3 * GROUP + 64]
    xg = w[:, b0 + 3 * GROUP + 64:b0 + 3 * GROUP + 160]
    wb = jnp.concatenate([rkv, _pad_cols(xw, LANES), _pad_cols(xa, LANES), _pad_cols(xg, LANES)], axis=1)
    c0 = b0 + 3 * GROUP + 160
    wc = w[:, c0:c0 + 3 * GROUP]
    d0 = c0 + 3 * GROUP
    qkv = w[:, d0:d0 + 2 * D_QK + GROUP]
    dxg = w[:, d0 + 2 * D_QK + GROUP:d0 + 2 * D_QK + GROUP + 16]
    og = w[:, d0 + 2 * D_QK + GROUP + 16:d0 + 2 * D_QK + 2 * GROUP + 16]
    wd = jnp.concatenate([qkv, og, _pad_cols(dxg, LANES)], axis=1)
    return [t.astype(BF16) for t in (wa, wb, wc, wd)]


def _split_mu(mu):
    rkv = mu[:3 * GROUP]
    pads = [jnp.pad(mu[3 * GROUP + lo:3 * GROUP + hi], (0, LANES - (hi - lo))) for lo, hi in ((0, 32), (32, 64), (64, 160))]
    return jnp.concatenate([rkv] + pads).reshape(1, B_W)


def kernel(x, mix_norm_g, w_in, a_ln_g, a_ln_b, a_ws, a_bs, b_mu, b_w0, b_w2, b_a0, b_a2, b_g2, b_kk, b_ka, b_rk, b_lnx_g, b_lnx_b, d_gate_w2, d_gate_b, d_norm_g, w_out, ffn_norm_g, w_up, conv_w, conv_b, w_down, final_norm_g):
    bsz, s, d = x.shape
    depth = w_in.shape[0]
    cos, sin = _rope_tables(s)
    outs = []
    for bi in range(bsz):
        xc = x[bi]
        for l in range(depth):
            h = _rmsnorm(xc, mix_norm_g[l], BF16)
            wa, wb, wc, wd = _split_w_in(w_in[l])
            pa = _matmul(h, wa, F32)
            pb = _matmul(h, wb, F32)
            pc = _matmul(h, wc, F32)
            pd = _matmul(h, wd, F32)
            ya = _gmlp(pa, a_ln_g[l], a_ln_b[l], a_ws[l], a_bs[l])
            yb = _rwkv(pb, _split_mu(b_mu[l]), b_w0[l], _pad_rows(b_w2[l], LANES).astype(BF16), b_a0[l],
                       _pad_rows(b_a2[l], LANES).astype(BF16), _pad_rows(b_g2[l], LANES).astype(BF16),
                       b_kk[l], b_ka[l], b_rk[l], b_lnx_g[l], b_lnx_b[l])
            yct = _moba(pc, cos, sin)
            yd = _gla(pd, _pad_rows(d_gate_w2[l], LANES).astype(BF16), d_gate_b[l], d_norm_g[l])
            xc, h2 = _outproj(ya, yb, yct, yd, w_out[l].astype(BF16), xc, ffn_norm_g[l])
            act = _ffn_up(h2, w_up, l, conv_w[l], conv_b[l])
            xc = _matmul_res(act, w_down, l, xc)
        outs.append(_rmsnorm(xc, final_norm_g, F32))
    return jnp.stack(outs, axis=0)
```
